```python
import jax, jax.numpy as jnp
from jax import lax
import numpy as np

D_MODEL = 1024
BATCH = 8
SEQ = 4096
DEPTH = 2

GRID_W = 64
CTX_LEN = 256
EPS = 1e-6
CONV_CH = 512
CONV_WIDTH = 31
N_HEADS = 8
QK_NOPE = 64
QK_ROPE = 32
V_HEAD = 64
Q_LORA = 256
KV_LORA = 128
ROPE_BASE = 10000.0
Q_BLOCK = 128
SC_CH = D_MODEL
SC_WIDTH = 3
N_EXPERTS = 64
TOP_K = 6
N_GROUPS = 8
TOPK_GROUPS = 4
D_EXPERT = 256
D_SHARED = 256
ROUTED_SCALE = 2.5
ROUTE_BLOCK = 128
N_EVEN = (DEPTH + 1) // 2
N_ODD = DEPTH // 2
IN_A = 2 * CONV_CH
IN_EVEN = IN_A + Q_LORA + KV_LORA + QK_ROPE
OUT_EVEN = CONV_CH + N_HEADS * V_HEAD

kernel_name = "hybrid_convmla_shortconv_moe_dit"


def rmsnorm(x, w):
    xf = x.astype(jnp.float32)
    y = xf * lax.rsqrt(jnp.mean(xf * xf, axis=-1, keepdims=True) + EPS)
    return (y * w.astype(jnp.float32)).astype(x.dtype)


def layernorm(x, w, b):
    xf = x.astype(jnp.float32)
    mu = jnp.mean(xf, axis=-1, keepdims=True)
    var = jnp.mean(jnp.square(xf - mu), axis=-1, keepdims=True)
    y = (xf - mu) * lax.rsqrt(var + EPS)
    return (y * w.astype(jnp.float32) + b.astype(jnp.float32)).astype(x.dtype)


def modulate(h, shift, scale):
    return h * (1 + scale) + shift


def axial_rope_tables(n):
    rows = n // GRID_W
    row = jnp.repeat(jnp.arange(rows), GRID_W).astype(jnp.float32)
    col = jnp.tile(jnp.arange(GRID_W), rows).astype(jnp.float32)
    half = QK_ROPE // 2
    inv = 1.0 / (ROPE_BASE ** (jnp.arange(0, half, 2, dtype=jnp.float32) / half))
    ang = jnp.stack([row[:, None] * inv, col[:, None] * inv], axis=1)
    return jnp.cos(ang), jnp.sin(ang)


def apply_rope(x, cos, sin):
    xr = x.reshape(x.shape[:-1] + (2, 2, QK_ROPE // 4))
    x1, x2 = xr[..., 0, :], xr[..., 1, :]
    out = jnp.stack([x1 * cos - x2 * sin, x2 * cos + x1 * sin], axis=-2)
    return out.reshape(x.shape).astype(x.dtype)


def dwconv(v, w):
    pad = w.shape[0] // 2
    return lax.conv_general_dilated(v, w.astype(v.dtype), (1,), [(pad, pad)],
                                    dimension_numbers=('NWC', 'WIO', 'NWC'),
                                    feature_group_count=v.shape[-1])


def conformer_conv(u, conv_w, conv_b, ln_w, ln_b):
    a, g = jnp.split(u, 2, axis=-1)
    y = dwconv(a * jax.nn.sigmoid(g), conv_w) + conv_b
    return jax.nn.silu(layernorm(y, ln_w, ln_b))


def mla_kv(ckv, k_rope, kv_norm_w, w_ukv, rope):
    kv = jnp.einsum('bnc,chd->bnhd', rmsnorm(ckv, kv_norm_w), w_ukv)
    k_nope, v = kv[..., :QK_NOPE], kv[..., QK_NOPE:]
    if rope is not None:
        k_rope = apply_rope(k_rope, rope[0], rope[1])
    k_rope = jnp.broadcast_to(k_rope[:, :, None, :], k_nope.shape[:-1] + (QK_ROPE,))
    return jnp.concatenate([k_nope, k_rope.astype(k_nope.dtype)], axis=-1), v


def mla_q(cq, q_norm_w, w_uq, rope):
    q = jnp.einsum('bnc,chd->bnhd', rmsnorm(cq, q_norm_w), w_uq)
    if rope is not None:
        q_rope = apply_rope(q[..., QK_NOPE:], rope[0][:, None], rope[1][:, None])
        q = jnp.concatenate([q[..., :QK_NOPE], q_rope], axis=-1)
    return q


def attend(q, k, v):
    scale = (QK_NOPE + QK_ROPE) ** -0.5
    s = jnp.einsum('bqhd,bkhd->bhqk', q, k).astype(jnp.float32) * scale
    p = jax.nn.softmax(s, axis=-1)
    return jnp.einsum('bhqk,bkhd->bqhd', p.astype(v.dtype), v)


def blocked_attention(q, k, v):
    B, n, H, dq = q.shape
    nb = n // Q_BLOCK
    qb = q.reshape(B, nb, Q_BLOCK, H, dq).transpose(1, 0, 2, 3, 4)
    out = lax.map(lambda qi: attend(qi, k, v), qb)
    return out.transpose(1, 0, 2, 3, 4).reshape(B, n, H, v.shape[-1])


def split_even(u):
    o1, o2, o3 = IN_A, IN_A + Q_LORA, IN_A + Q_LORA + KV_LORA
    return u[..., :o1], u[..., o1:o2], u[..., o2:o3], u[..., o3:]


def even_mixer(a, ac, w_in, conv_w, conv_b, ln_w, ln_b, q_norm_w, kv_norm_w, w_uq, w_ukv, w_out, rope, ctx_out):
    B, n, _ = a.shape
    u_a, cq, ckv, kr = split_even(a @ w_in)
    if ctx_out:
        uc_a, cq_c, ckv_c, kr_c = split_even(ac @ w_in)
    else:
        uc_kv = ac @ w_in[:, IN_A + Q_LORA:]
        ckv_c, kr_c = uc_kv[..., :KV_LORA], uc_kv[..., KV_LORA:]
    k_c, v_c = mla_kv(ckv_c, kr_c, kv_norm_w, w_ukv, None)
    k_l, v_l = mla_kv(ckv, kr, kv_norm_w, w_ukv, rope)
    q_l = mla_q(cq, q_norm_w, w_uq, rope)
    o_b = blocked_attention(q_l, jnp.concatenate([k_c, k_l], axis=1), jnp.concatenate([v_c, v_l], axis=1))
    o_a = conformer_conv(u_a, conv_w, conv_b, ln_w, ln_b)
    y = jnp.concatenate([o_a, o_b.reshape(B, n, N_HEADS * V_HEAD)], axis=-1) @ w_out
    if not ctx_out:
        return y, None
    o_bc = attend(mla_q(cq_c, q_norm_w, w_uq, None), k_c, v_c)
    o_ac = conformer_conv(uc_a, conv_w, conv_b, ln_w, ln_b)
    yc = jnp.concatenate([o_ac, o_bc.reshape(B, ac.shape[1], N_HEADS * V_HEAD)], axis=-1) @ w_out
    return y, yc


def short_conv_mixer(h, w_in, conv_w, w_out):
    b, cg, xt = jnp.split(h @ w_in, 3, axis=-1)
    return (b * dwconv(cg * xt, conv_w)) @ w_out


def moe_ffn(h, router_w, router_bias, w_gate, w_up, w_down, sw_gate, sw_up, sw_down):
    T, D = h.shape
    scores = jax.nn.sigmoid((h @ router_w).astype(jnp.float32))
    sel = scores + router_bias.astype(jnp.float32)
    gscore = lax.top_k(sel.reshape(T, N_GROUPS, N_EXPERTS // N_GROUPS), 2)[0].sum(-1)
    _, gidx = lax.top_k(gscore, TOPK_GROUPS)
    gmask = jnp.any(gidx[..., None] == jnp.arange(N_GROUPS), axis=1)
    emask = jnp.repeat(gmask, N_EXPERTS // N_GROUPS, axis=1)
    _, eidx = lax.top_k(jnp.where(emask, sel, -jnp.inf), TOP_K)
    wts = jnp.take_along_axis(scores, eidx, axis=1)
    wts = wts / jnp.sum(wts, axis=-1, keepdims=True) * ROUTED_SCALE
    TK = T * TOP_K
    flat_e = eidx.reshape(-1)
    flat_tok = jnp.arange(TK, dtype=jnp.int32) // TOP_K
    order = jnp.argsort(flat_e)
    se, stok, sw = flat_e[order], flat_tok[order], wts.reshape(-1)[order]
    counts = jnp.bincount(flat_e, length=N_EXPERTS)
    starts = jnp.cumsum(counts) - counts
    pcounts = ((counts + ROUTE_BLOCK - 1) // ROUTE_BLOCK) * ROUTE_BLOCK
    pends = jnp.cumsum(pcounts)
    pstarts = pends - pcounts
    dest = pstarts[se] + (jnp.arange(TK) - starts[se])
    NB = -(-TK // ROUTE_BLOCK) + N_EXPERTS
    P = NB * ROUTE_BLOCK
    row_tok = jnp.full((P,), T, dtype=jnp.int32).at[dest].set(stok)
    row_w = jnp.zeros((P,), dtype=h.dtype).at[dest].set(sw.astype(h.dtype))
    block_e = jnp.minimum(jnp.searchsorted(pends, jnp.arange(NB) * ROUTE_BLOCK, side='right'), N_EXPERTS - 1)
    h_pad = jnp.concatenate([h, jnp.zeros((1, D), h.dtype)], axis=0)

    def run_block(args):
        toks, bw, e = args
        xb = h_pad[toks]
        act = jax.nn.silu(xb @ w_gate[e]) * (xb @ w_up[e])
        return (act @ w_down[e]) * bw[:, None]

    yb = lax.map(run_block, (row_tok.reshape(NB, ROUTE_BLOCK), row_w.reshape(NB, ROUTE_BLOCK), block_e))
    routed = jax.ops.segment_sum(yb.reshape(P, D), row_tok, num_segments=T + 1)[:T]
    shared = (jax.nn.silu(h @ sw_gate) * (h @ sw_up)) @ sw_down
    return (routed + shared).astype(h.dtype)


def setup_inputs(seed: int = 0) -> dict:
    key = jax.random.key(seed)
    ks = iter(jax.random.split(key, 40))

    def nrm(shape, scale):
        return jax.random.normal(next(ks), shape, jnp.float32) * scale

    def gain(shape):
        return 1.0 + nrm(shape, 0.05)

    D = D_MODEL
    return {
        "x": nrm((BATCH, SEQ, D), 1.0),
        "c": nrm((BATCH, D), 1.0),
        "ctx": nrm((BATCH, CTX_LEN, D), 1.0),
        "c_ctx": nrm((D,), 1.0),
        "ada_w": nrm((DEPTH, D, 6 * D), 0.5 * D ** -0.5),
        "ada_b": nrm((DEPTH, 6 * D), 0.02),
        "norm_mix_w": gain((DEPTH, D)),
        "norm_ffn_w": gain((DEPTH, D)),
        "ev_w_in": nrm((N_EVEN, D, IN_EVEN), D ** -0.5),
        "ev_conv_w": nrm((N_EVEN, CONV_WIDTH, 1, CONV_CH), CONV_WIDTH ** -0.5),
        "ev_conv_b": nrm((N_EVEN, CONV_CH), 0.02),
        "ev_ln_w": gain((N_EVEN, CONV_CH)),
        "ev_ln_b": nrm((N_EVEN, CONV_CH), 0.02),
        "ev_q_norm_w": gain((N_EVEN, Q_LORA)),
        "ev_kv_norm_w": gain((N_EVEN, KV_LORA)),
        "ev_w_uq": nrm((N_EVEN, Q_LORA, N_HEADS, QK_NOPE + QK_ROPE), Q_LORA ** -0.5),
        "ev_w_ukv": nrm((N_EVEN, KV_LORA, N_HEADS, QK_NOPE + V_HEAD), KV_LORA ** -0.5),
        "ev_w_out": nrm((N_EVEN, OUT_EVEN, D), OUT_EVEN ** -0.5),
        "od_w_in": nrm((N_ODD, D, 3 * SC_CH), D ** -0.5),
        "od_conv_w": nrm((N_ODD, SC_WIDTH, 1, SC_CH), SC_WIDTH ** -0.5),
        "od_w_out": nrm((N_ODD, SC_CH, D), SC_CH ** -0.5),
        "router_w": nrm((DEPTH, D, N_EXPERTS), D ** -0.5),
        "router_bias": nrm((DEPTH, N_EXPERTS), 0.01),
        "exp_w_gate": nrm((DEPTH, N_EXPERTS, D, D_EXPERT), D ** -0.5),
        "exp_w_up": nrm((DEPTH, N_EXPERTS, D, D_EXPERT), D ** -0.5),
        "exp_w_down": nrm((DEPTH, N_EXPERTS, D_EXPERT, D), D_EXPERT ** -0.5),
        "sh_w_gate": nrm((DEPTH, D, D_SHARED), D ** -0.5),
        "sh_w_up": nrm((DEPTH, D, D_SHARED), D ** -0.5),
        "sh_w_down": nrm((DEPTH, D_SHARED, D), D_SHARED ** -0.5),
        "final_norm_w": gain((D,)),
    }


def reference(x, c, ctx, c_ctx, ada_w, ada_b, norm_mix_w, norm_ffn_w, ev_w_in, ev_conv_w, ev_conv_b, ev_ln_w,
              ev_ln_b, ev_q_norm_w, ev_kv_norm_w, ev_w_uq, ev_w_ukv, ev_w_out, od_w_in, od_conv_w, od_w_out,
              router_w, router_bias, exp_w_gate, exp_w_up, exp_w_down, sh_w_gate, sh_w_up, sh_w_down,
              final_norm_w):
    B, N, D = x.shape
    rope = axial_rope_tables(N)
    h, hc = x, ctx
    for i in range(DEPTH):
        reads_ctx = i % 2 == 0
        ctx_out = any(j % 2 == 0 for j in range(i + 1, DEPTH))
        m = jax.nn.silu(c) @ ada_w[i] + ada_b[i]
        sh1, sc1, g1, sh2, sc2, g2 = jnp.split(m[:, None, :], 6, axis=-1)
        a = modulate(rmsnorm(h, norm_mix_w[i]), sh1, sc1)
        if reads_ctx or ctx_out:
            mc = jax.nn.silu(c_ctx) @ ada_w[i] + ada_b[i]
            csh1, csc1, cg1, csh2, csc2, cg2 = jnp.split(mc, 6)
            ac = modulate(rmsnorm(hc, norm_mix_w[i]), csh1, csc1)
        if i % 2 == 0:
            e = i // 2
            y, yc = even_mixer(a, ac, ev_w_in[e], ev_conv_w[e], ev_conv_b[e], ev_ln_w[e], ev_ln_b[e],
                               ev_q_norm_w[e], ev_kv_norm_w[e], ev_w_uq[e], ev_w_ukv[e], ev_w_out[e], rope, ctx_out)
        else:
            o = i // 2
            y = short_conv_mixer(a, od_w_in[o], od_conv_w[o], od_w_out[o])
            yc = short_conv_mixer(ac, od_w_in[o], od_conv_w[o], od_w_out[o]) if ctx_out else None
        h = h + g1 * y
        f = modulate(rmsnorm(h, norm_ffn_w[i]), sh2, sc2)
        moe_args = (router_w[i], router_bias[i], exp_w_gate[i], exp_w_up[i], exp_w_down[i],
                    sh_w_gate[i], sh_w_up[i], sh_w_down[i])
        if ctx_out:
            hc = hc + cg1 * yc
            fc = modulate(rmsnorm(hc, norm_ffn_w[i]), csh2, csc2)
            nc = fc.shape[0] * fc.shape[1]
            out = moe_ffn(jnp.concatenate([fc.reshape(-1, D), f.reshape(-1, D)], axis=0), *moe_args)
            hc = hc + cg2 * out[:nc].reshape(hc.shape)
            h = h + g2 * out[nc:].reshape(h.shape)
        else:
            h = h + g2 * moe_ffn(f.reshape(-1, D), *moe_args).reshape(h.shape)
    return rmsnorm(h, final_norm_w)
```

```python
import functools

import jax
import jax.numpy as jnp
import numpy as np
from jax import lax
from jax.experimental import pallas as pl
from jax.experimental.pallas import tpu as pltpu

F32 = jnp.float32
BF16 = jnp.bfloat16

EPS = 1e-6
GRID_W = 64
CONV_CH = 512
CONV_WIDTH = 31
N_HEADS = 8
QK_NOPE = 64
QK_ROPE = 32
V_HEAD = 64
Q_LORA = 256
KV_LORA = 128
ROPE_BASE = 10000.0
N_EXPERTS = 64
TOP_K = 6
N_GROUPS = 8
TOPK_GROUPS = 4
ROUTED_SCALE = 2.5
ROUTE_BLOCK = 128

HEAD_LANES = 128
VMEM_LIMIT = 56 * 1024 * 1024


def _cparams(*sem):
    return pltpu.CompilerParams(dimension_semantics=sem, vmem_limit_bytes=VMEM_LIMIT)


def _dot(a, b):
    return jnp.dot(a, b, preferred_element_type=F32)


def _dot_nt(a, b):
    return lax.dot_general(a, b, (((1,), (1,)), ((), ())), preferred_element_type=F32)


def _rms(x, w):
    return x * lax.rsqrt(jnp.mean(x * x, axis=-1, keepdims=True) + EPS) * w


def _silu(x):
    return x * jax.nn.sigmoid(x)


def _ada_kernel(c_ref, w_ref, b_ref, o_ref):
    c = c_ref[...]
    o_ref[0] = _dot(_silu(c), w_ref[0]) + b_ref[0]


def _ada(c16, ada_w, ada_b):
    depth, d, n6 = ada_w.shape
    tn = 1536
    return pl.pallas_call(
        _ada_kernel,
        grid=(depth, n6 // tn),
        in_specs=[
            pl.BlockSpec((16, d), lambda l, j: (0, 0)),
            pl.BlockSpec((1, d, tn), lambda l, j: (l, 0, j)),
            pl.BlockSpec((1, 1, tn), lambda l, j: (l, 0, j)),
        ],
        out_specs=pl.BlockSpec((1, 16, tn), lambda l, j: (l, 0, j)),
        out_shape=jax.ShapeDtypeStruct((depth, 16, n6), F32),
        compiler_params=_cparams("arbitrary", "arbitrary"),
        name="ada",
    )(c16, ada_w, ada_b.reshape(depth, 1, n6))


def _inproj0_kernel(x_ref, mod_ref, nw_ref, win_ref, qnw_ref, kvnw_ref, wq_ref, wkv_ref, e_ref,
                    ck_ref, cq_ref, *outs, latent, d):
    x = x_ref[...]
    mod = mod_ref[0]
    a = (_rms(x, nw_ref[...]) * (1.0 + mod[:, d:2 * d]) + mod[:, 0:d]).astype(BF16)
    u = _dot(a, win_ref[...])
    if latent:
        glu_ref, q_ref, k_ref, v_ref = outs
        glu_ref[...] = u[:, 0:CONV_CH] * jax.nn.sigmoid(u[:, CONV_CH:2 * CONV_CH])
        o = 2 * CONV_CH
        cqn = _rms(u[:, o:o + Q_LORA], qnw_ref[...]).astype(BF16)
        o += Q_LORA
        cosq = cq_ref[:, 0:HEAD_LANES]
        sinq = cq_ref[:, HEAD_LANES:2 * HEAD_LANES]
        nq = N_HEADS * HEAD_LANES
        qa = _dot(cqn, wq_ref[:, 0:nq])
        qb = _dot(cqn, wq_ref[:, nq:2 * nq])
        for h in range(N_HEADS):
            sl = slice(h * HEAD_LANES, (h + 1) * HEAD_LANES)
            q_ref[:, sl] = (qa[:, sl] * cosq + qb[:, sl] * sinq).astype(BF16)
    else:
        k_ref, v_ref = outs
        o = 0
    nk = N_HEADS * HEAD_LANES
    kvn = _rms(u[:, o:o + KV_LORA], kvnw_ref[...]).astype(BF16)
    o += KV_LORA
    prod = u[:, o:o + HEAD_LANES] * ck_ref[...]
    krot = (prod + pltpu.roll(prod, HEAD_LANES - QK_ROPE, 1)).astype(BF16)
    k = _dot(kvn, wkv_ref[:, 0:nk]) + _dot(krot, e_ref[...])
    k_ref[...] = k.astype(BF16)
    lane = lax.broadcasted_iota(jnp.int32, (1, nk), 1)
    ones = jnp.where(lane % HEAD_LANES == V_HEAD, 1.0, 0.0).astype(F32)
    v_ref[...] = (_dot(kvn, wkv_ref[:, nk:2 * nk]) + ones).astype(BF16)


def _inproj0(x2, mod3, mod_row_fn, nw, win, qnw, kvnw, wq, wkv, emat, ck, cq, *, latent, tm, seq):
    t, d = x2.shape
    nk = N_HEADS * HEAD_LANES
    tiles_per_seq = seq // tm
    full = lambda i: (0, 0)
    in_specs = [
        pl.BlockSpec((tm, d), lambda i: (i, 0)),
        pl.BlockSpec((1, 1, 6 * d), lambda i: (mod_row_fn(i), 0, 0)),
        pl.BlockSpec((1, d), full),
        pl.BlockSpec(win.shape, full),
        pl.BlockSpec((1, Q_LORA), full),
        pl.BlockSpec((1, KV_LORA), full),
        pl.BlockSpec(wq.shape, full),
        pl.BlockSpec(wkv.shape, full),
        pl.BlockSpec(emat.shape, full),
        pl.BlockSpec((tm, HEAD_LANES), lambda i: (i % tiles_per_seq, 0)),
        pl.BlockSpec((tm, 2 * HEAD_LANES), lambda i: (i % tiles_per_seq, 0)),
    ]
    row = lambda i: (i, 0)
    out_specs = [pl.BlockSpec((tm, nk), row), pl.BlockSpec((tm, nk), row)]
    out_shape = [jax.ShapeDtypeStruct((t, nk), BF16), jax.ShapeDtypeStruct((t, nk), BF16)]
    if latent:
        out_specs = [pl.BlockSpec((tm, CONV_CH), row), pl.BlockSpec((tm, nk), row)] + out_specs
        out_shape = [jax.ShapeDtypeStruct((t, CONV_CH), F32), jax.ShapeDtypeStruct((t, nk), BF16)] + out_shape
    return pl.pallas_call(
        functools.partial(_inproj0_kernel, latent=latent, d=d),
        grid=(t // tm,),
        in_specs=in_specs,
        out_specs=out_specs,
        out_shape=out_shape,
        compiler_params=_cparams("arbitrary"),
        name="inproj0_lat" if latent else "inproj0_ctx",
    )(x2, mod3, nw, win, qnw, kvnw, wq, wkv, emat, ck, cq)


CONV_HALO = 16
CONV_ROWS = 64


def _conv_kernel(main_ref, prev_ref, next_ref, w_ref, b_ref, lnw_ref, lnb_ref, o_ref, win_ref, *,
                 tn, tiles_per_seq):
    i = pl.program_id(0)
    first = (i % tiles_per_seq) == 0
    last = (i % tiles_per_seq) == tiles_per_seq - 1
    win_ref[0:CONV_HALO, :] = jnp.where(first, 0.0, prev_ref[...])
    win_ref[CONV_HALO:CONV_HALO + tn, :] = main_ref[...]
    win_ref[CONV_HALO + tn:, :] = jnp.where(last, 0.0, next_ref[...])
    pad = CONV_WIDTH // 2
    for c in range(tn // CONV_ROWS):
        base = c * CONV_ROWS + CONV_HALO - pad
        acc = win_ref[base:base + CONV_ROWS, :] * w_ref[0:1, :]
        for k in range(1, CONV_WIDTH):
            acc = acc + win_ref[base + k:base + k + CONV_ROWS, :] * w_ref[k:k + 1, :]
        y = acc + b_ref[...]
        mu = jnp.mean(y, axis=-1, keepdims=True)
        yc = y - mu
        var = jnp.mean(yc * yc, axis=-1, keepdims=True)
        yn = yc * lax.rsqrt(var + EPS) * lnw_ref[...] + lnb_ref[...]
        o_ref[c * CONV_ROWS:(c + 1) * CONV_ROWS, :] = _silu(yn).astype(BF16)


def _conv(glu, w, b, lnw, lnb, *, tn, seq):
    t, ch = glu.shape
    tiles_per_seq = seq // tn
    hb = tn // CONV_HALO
    nhb = t // CONV_HALO
    full = lambda i: (0, 0)
    return pl.pallas_call(
        functools.partial(_conv_kernel, tn=tn, tiles_per_seq=tiles_per_seq),
        grid=(t // tn,),
        in_specs=[
            pl.BlockSpec((tn, ch), lambda i: (i, 0)),
            pl.BlockSpec((CONV_HALO, ch), lambda i: (jnp.maximum(i * hb - 1, 0), 0)),
            pl.BlockSpec((CONV_HALO, ch), lambda i: (jnp.minimum((i + 1) * hb, nhb - 1), 0)),
            pl.BlockSpec(w.shape, full),
            pl.BlockSpec((1, ch), full),
            pl.BlockSpec((1, ch), full),
            pl.BlockSpec((1, ch), full),
        ],
        out_specs=pl.BlockSpec((tn, ch), lambda i: (i, 0)),
        out_shape=jax.ShapeDtypeStruct((t, ch), BF16),
        scratch_shapes=[pltpu.VMEM((tn + 2 * CONV_HALO, ch), F32)],
        compiler_params=_cparams("arbitrary"),
        name="conv",
    )(glu, glu, glu, w, b, lnw, lnb)


ATT_KC = 512


def _attn_kernel(q_ref, kc_ref, kl_ref, vc_ref, vl_ref, o_ref, s_ref, *, nctx, nlat):
    outs = []
    for hh in range(2):
        sl = slice(hh * HEAD_LANES, (hh + 1) * HEAD_LANES)
        q = q_ref[0, :, sl]
        s = _dot_nt(q, kc_ref[0, :, sl])
        s_ref[:, 0:nctx] = s
        m = jnp.max(s, axis=-1, keepdims=True)
        for c in range(nlat // ATT_KC):
            s = _dot_nt(q, kl_ref[0, c * ATT_KC:(c + 1) * ATT_KC, sl])
            s_ref[:, nctx + c * ATT_KC:nctx + (c + 1) * ATT_KC] = s
            m = jnp.maximum(m, jnp.max(s, axis=-1, keepdims=True))
        p = jnp.exp(s_ref[:, 0:nctx] - m).astype(BF16)
        acc = _dot(p, vc_ref[0, :, sl])
        for c in range(nlat // ATT_KC):
            p = jnp.exp(s_ref[:, nctx + c * ATT_KC:nctx + (c + 1) * ATT_KC] - m).astype(BF16)
            acc = acc + _dot(p, vl_ref[0, c * ATT_KC:(c + 1) * ATT_KC, sl])
        outs.append(acc[:, 0:V_HEAD] / acc[:, V_HEAD:V_HEAD + 1])
    o_ref[0] = jnp.concatenate(outs, axis=-1).astype(BF16)


def _attn(q, kc, kl, vc, vl, *, tq):
    b, n, _ = q.shape
    nctx = kc.shape[1]
    hp = N_HEADS // 2
    w2 = 2 * HEAD_LANES
    return pl.pallas_call(
        functools.partial(_attn_kernel, nctx=nctx, nlat=n),
        grid=(b, hp, n // tq),
        in_specs=[
            pl.BlockSpec((1, tq, w2), lambda bi, j, i: (bi, i, j)),
            pl.BlockSpec((1, nctx, w2), lambda bi, j, i: (bi, 0, j)),
            pl.BlockSpec((1, n, w2), lambda bi, j, i: (bi, 0, j)),
            pl.BlockSpec((1, nctx, w2), lambda bi, j, i: (bi, 0, j)),
            pl.BlockSpec((1, n, w2), lambda bi, j, i: (bi, 0, j)),
        ],
        out_specs=pl.BlockSpec((1, tq, 2 * V_HEAD), lambda bi, j, i: (bi, i, j)),
        out_shape=jax.ShapeDtypeStruct((b, n, N_HEADS * V_HEAD), BF16),
        scratch_shapes=[pltpu.VMEM((tq, nctx + n), F32)],
        compiler_params=_cparams("arbitrary", "arbitrary", "arbitrary"),
        name="attn",
    )(q, kc, kl, vc, vl)


def _tail(y, x, mod, d, nfw_ref, swg_ref, swu_ref, swd_ref, rwh_ref, rwl_ref, f_ref, base_ref, sc_ref):
    g1 = mod[:, 2 * d:3 * d]
    sh2 = mod[:, 3 * d:4 * d]
    sc2 = mod[:, 4 * d:5 * d]
    g2 = mod[:, 5 * d:6 * d]
    h1 = x + g1 * y
    f = _rms(h1, nfw_ref[...]) * (1.0 + sc2) + sh2
    f_ref[...] = f
    fb = f.astype(BF16)
    act = (_silu(_dot(fb, swg_ref[...])) * _dot(fb, swu_ref[...])).astype(BF16)
    base_ref[...] = h1 + g2 * _dot(act, swd_ref[...])
    flo = (f - fb.astype(F32)).astype(BF16)
    logits = _dot(fb, rwh_ref[...]) + (_dot(flo, rwh_ref[...]) + _dot(fb, rwl_ref[...]))
    sc_ref[...] = jax.nn.sigmoid(logits)


def _mix0_kernel(oa_ref, ob_ref, x_ref, mod_ref, woa_ref, wob_ref, *rest, d):
    y = _dot(oa_ref[...], woa_ref[...]) + _dot(ob_ref[...], wob_ref[...])
    _tail(y, x_ref[...], mod_ref[0], d, *rest)


def _mix1_kernel(b_ref, z_ref, zp_ref, zn_ref, x_ref, mod_ref, cw_ref, wo_ref, *rest, d, tm, tiles_per_seq):
    i = pl.program_id(0)
    first = (i % tiles_per_seq) == 0
    last = (i % tiles_per_seq) == tiles_per_seq - 1
    z = z_ref[...].astype(F32)
    rowi = lax.broadcasted_iota(jnp.int32, (tm, 1), 0)
    zprev_row = jnp.where(first, 0.0, zp_ref[7:8, :].astype(F32))
    znext_row = jnp.where(last, 0.0, zn_ref[0:1, :].astype(F32))
    zm = jnp.where(rowi == 0, zprev_row, pltpu.roll(z, 1, 0))
    zp = jnp.where(rowi == tm - 1, znext_row, pltpu.roll(z, tm - 1, 0))
    conv = zm * cw_ref[0:1, :] + z * cw_ref[1:2, :] + zp * cw_ref[2:3, :]
    v = (b_ref[...].astype(F32) * conv).astype(BF16)
    _tail(_dot(v, wo_ref[...]), x_ref[...], mod_ref[0], d, *rest)


def _tail_specs(t, d, tm, nfw, swg, swu, swd, rwh, rwl):
    full = lambda i: (0, 0)
    row = lambda i: (i, 0)
    ne = rwh.shape[1]
    in_specs = [pl.BlockSpec((1, d), full), pl.BlockSpec(swg.shape, full), pl.BlockSpec(swu.shape, full),
                pl.BlockSpec(swd.shape, full), pl.BlockSpec(rwh.shape, full), pl.BlockSpec(rwl.shape, full)]
    out_specs = [pl.BlockSpec((tm, d), row), pl.BlockSpec((tm, d), row), pl.BlockSpec((tm, ne), row)]
    out_shape = [jax.ShapeDtypeStruct((t, d), F32), jax.ShapeDtypeStruct((t, d), F32),
                 jax.ShapeDtypeStruct((t, ne), F32)]
    return in_specs, out_specs, out_shape


def _mix0(oa, ob, x2, mod3, woa, wob, tail_w, *, tm, seq):
    t, d = x2.shape
    full = lambda i: (0, 0)
    row = lambda i: (i, 0)
    tin, out_specs, out_shape = _tail_specs(t, d, tm, *tail_w)
    return pl.pallas_call(
        functools.partial(_mix0_kernel, d=d),
        grid=(t // tm,),
        in_specs=[pl.BlockSpec((tm, oa.shape[1]), row), pl.BlockSpec((tm, ob.shape[1]), row),
                  pl.BlockSpec((tm, d), row),
                  pl.BlockSpec((1, 1, 6 * d), lambda i: ((i * tm) // seq, 0, 0)),
                  pl.BlockSpec(woa.shape, full), pl.BlockSpec(wob.shape, full)] + tin,
        out_specs=out_specs,
        out_shape=out_shape,
        compiler_params=_cparams("arbitrary"),
        name="mix0_tail",
    )(oa, ob, x2, mod3, woa, wob, *tail_w)


def _mix1(bq, z, x2, mod3, cw, wo, tail_w, *, tm, seq):
    t, d = x2.shape
    full = lambda i: (0, 0)
    row = lambda i: (i, 0)
    hb = tm // 8
    nhb = t // 8
    tin, out_specs, out_shape = _tail_specs(t, d, tm, *tail_w)
    return pl.pallas_call(
        functools.partial(_mix1_kernel, d=d, tm=tm, tiles_per_seq=seq // tm),
        grid=(t // tm,),
        in_specs=[pl.BlockSpec((tm, d), row), pl.BlockSpec((tm, d), row),
                  pl.BlockSpec((8, d), lambda i: (jnp.maximum(i * hb - 1, 0), 0)),
                  pl.BlockSpec((8, d), lambda i: (jnp.minimum((i + 1) * hb, nhb - 1), 0)),
                  pl.BlockSpec((tm, d), row),
                  pl.BlockSpec((1, 1, 6 * d), lambda i: ((i * tm) // seq, 0, 0)),
                  pl.BlockSpec(cw.shape, full), pl.BlockSpec(wo.shape, full)] + tin,
        out_specs=out_specs,
        out_shape=out_shape,
        compiler_params=_cparams("arbitrary"),
        name="mix1_tail",
    )(bq, z, z, z, x2, mod3, cw, wo, *tail_w)


def _inproj1_kernel(x_ref, mod_ref, nw_ref, win_ref, b_ref, z_ref, *, d):
    mod = mod_ref[0]
    a = (_rms(x_ref[...], nw_ref[...]) * (1.0 + mod[:, d:2 * d]) + mod[:, 0:d]).astype(BF16)
    u = _dot(a, win_ref[...])
    b_ref[...] = u[:, 0:d].astype(BF16)
    z_ref[...] = (u[:, d:2 * d] * u[:, 2 * d:3 * d]).astype(F32)


def _inproj1(x2, mod3, nw, win, *, tm, seq):
    t, d = x2.shape
    full = lambda i: (0, 0)
    row = lambda i: (i, 0)
    return pl.pallas_call(
        functools.partial(_inproj1_kernel, d=d),
        grid=(t // tm,),
        in_specs=[pl.BlockSpec((tm, d), row),
                  pl.BlockSpec((1, 1, 6 * d), lambda i: ((i * tm) // seq, 0, 0)),
                  pl.BlockSpec((1, d), full), pl.BlockSpec(win.shape, full)],
        out_specs=[pl.BlockSpec((tm, d), row), pl.BlockSpec((tm, d), row)],
        out_shape=[jax.ShapeDtypeStruct((t, d), BF16), jax.ShapeDtypeStruct((t, d), F32)],
        compiler_params=_cparams("arbitrary"),
        name="inproj1",
    )(x2, mod3, nw, win)


def _idx_copy(idx_hbm, idx_smem, sem, j, slot):
    return pltpu.make_async_copy(idx_hbm.at[pl.ds(j, 1)], idx_smem.at[slot], sem.at[slot])


def _issue_rows(src_hbm, idx_smem, islot, buf, sem, bslot, nrows):
    def body(r, carry):
        t = idx_smem[islot, 0, r]
        pltpu.make_async_copy(src_hbm.at[pl.ds(t, 1)], buf.at[bslot, pl.ds(r, 1)], sem.at[bslot]).start()
        return carry
    lax.fori_loop(0, nrows, body, 0, unroll=8)


def _wait_rows(src_hbm, buf, sem, bslot, nrows):
    pltpu.make_async_copy(src_hbm.at[pl.ds(0, nrows)], buf.at[bslot], sem.at[bslot]).wait()


def _gather_step(src_hbm, idx_hbm, idx_smem, isem, buf, rsem, nrows):
    i = pl.program_id(0)
    n = pl.num_programs(0)
    slot = i % 2

    @pl.when(i == 0)
    def _():
        _idx_copy(idx_hbm, idx_smem, isem, 0, 0).start()
        _idx_copy(idx_hbm, idx_smem, isem, 0, 0).wait()
        _issue_rows(src_hbm, idx_smem, 0, buf, rsem, 0, nrows)

        @pl.when(n > 1)
        def _():
            _idx_copy(idx_hbm, idx_smem, isem, 1, 1).start()

    @pl.when(i + 1 < n)
    def _():
        _idx_copy(idx_hbm, idx_smem, isem, i + 1, 1 - slot).wait()
        _issue_rows(src_hbm, idx_smem, 1 - slot, buf, rsem, 1 - slot, nrows)

        @pl.when(i + 2 < n)
        def _():
            _idx_copy(idx_hbm, idx_smem, isem, i + 2, slot).start()

    _wait_rows(src_hbm, buf, rsem, slot, nrows)
    return slot


def _experts_kernel(be_ref, tok_hbm, f_hbm, rw_ref, wg_ref, wu_ref, wd_ref, y_ref, idx_smem, isem, xbuf, rsem):
    slot = _gather_step(f_hbm, tok_hbm, idx_smem, isem, xbuf, rsem, ROUTE_BLOCK)
    x = xbuf[slot].astype(BF16)
    act = (_silu(_dot(x, wg_ref[0])) * _dot(x, wu_ref[0])).astype(BF16)
    y_ref[...] = _dot(act, wd_ref[0]) * rw_ref[...]


def _experts(block_e, row_tok, row_w, f, wg, wu, wd):
    nb = block_e.shape[0]
    t, d = f.shape
    de = wg.shape[2]
    grid_spec = pltpu.PrefetchScalarGridSpec(
        num_scalar_prefetch=1,
        grid=(nb,),
        in_specs=[
            pl.BlockSpec(memory_space=pl.ANY),
            pl.BlockSpec(memory_space=pl.ANY),
            pl.BlockSpec((ROUTE_BLOCK, 1), lambda i, be: (i, 0)),
            pl.BlockSpec((1, d, de), lambda i, be: (be[i], 0, 0)),
            pl.BlockSpec((1, d, de), lambda i, be: (be[i], 0, 0)),
            pl.BlockSpec((1, de, d), lambda i, be: (be[i], 0, 0)),
        ],
        out_specs=pl.BlockSpec((ROUTE_BLOCK, d), lambda i, be: (i, 0)),
        scratch_shapes=[
            pltpu.SMEM((2, 1, ROUTE_BLOCK), jnp.int32),
            pltpu.SemaphoreType.DMA((2,)),
            pltpu.VMEM((2, ROUTE_BLOCK, d), F32),
            pltpu.SemaphoreType.DMA((2,)),
        ],
    )
    return pl.pallas_call(
        _experts_kernel,
        grid_spec=grid_spec,
        out_shape=jax.ShapeDtypeStruct((nb * ROUTE_BLOCK, d), F32),
        compiler_params=_cparams("arbitrary"),
        name="experts",
    )(block_e, row_tok.reshape(nb, ROUTE_BLOCK), f, row_w.reshape(nb * ROUTE_BLOCK, 1), wg, wu, wd)


COMB_TOK = 128


def _combine_kernel(pos_hbm, y_hbm, base_ref, mod_ref, fnw_ref, o_ref, idx_smem, isem, buf, rsem, *, d, final):
    slot = _gather_step(y_hbm, pos_hbm, idx_smem, isem, buf, rsem, TOP_K * COMB_TOK)
    routed = buf[slot, 0:COMB_TOK, :]
    for k in range(1, TOP_K):
        routed = routed + buf[slot, k * COMB_TOK:(k + 1) * COMB_TOK, :]
    h = base_ref[...] + mod_ref[0][:, 5 * d:6 * d] * routed
    if final:
        h = _rms(h, fnw_ref[...])
    o_ref[...] = h


def _combine(pos, y, base, mod3, fnw, *, seq, final):
    t, d = base.shape
    nt = t // COMB_TOK
    full = lambda i: (0, 0)
    row = lambda i: (i, 0)
    return pl.pallas_call(
        functools.partial(_combine_kernel, d=d, final=final),
        grid=(nt,),
        in_specs=[
            pl.BlockSpec(memory_space=pl.ANY),
            pl.BlockSpec(memory_space=pl.ANY),
            pl.BlockSpec((COMB_TOK, d), row),
            pl.BlockSpec((1, 1, 6 * d), lambda i: ((i * COMB_TOK) // seq, 0, 0)),
            pl.BlockSpec((1, d), full),
        ],
        out_specs=pl.BlockSpec((COMB_TOK, d), row),
        out_shape=jax.ShapeDtypeStruct((t, d), F32),
        scratch_shapes=[
            pltpu.SMEM((2, 1, TOP_K * COMB_TOK), jnp.int32),
            pltpu.SemaphoreType.DMA((2,)),
            pltpu.VMEM((2, TOP_K * COMB_TOK, d), F32),
            pltpu.SemaphoreType.DMA((2,)),
        ],
        compiler_params=_cparams("arbitrary"),
        name="combine_final" if final else "combine",
    )(pos, y, base, mod3, fnw)


def _route(scores, router_bias):
    t = scores.shape[0]
    sel = scores + router_bias.astype(F32)
    gscore = lax.top_k(sel.reshape(t, N_GROUPS, N_EXPERTS // N_GROUPS), 2)[0].sum(-1)
    _, gidx = lax.top_k(gscore, TOPK_GROUPS)
    gmask = jnp.any(gidx[..., None] == jnp.arange(N_GROUPS), axis=1)
    emask = jnp.repeat(gmask, N_EXPERTS // N_GROUPS, axis=1)
    _, eidx = lax.top_k(jnp.where(emask, sel, -jnp.inf), TOP_K)
    wts = jnp.take_along_axis(scores, eidx, axis=1)
    wts = wts / jnp.sum(wts, axis=-1, keepdims=True) * ROUTED_SCALE
    return eidx, wts


def _dispatch_plan(eidx, wts):
    t = eidx.shape[0]
    tk = t * TOP_K
    flat_e = eidx.reshape(-1)
    flat_tok = jnp.arange(tk, dtype=jnp.int32) // TOP_K
    order = jnp.argsort(flat_e)
    se = flat_e[order]
    counts = jnp.bincount(flat_e, length=N_EXPERTS)
    starts = jnp.cumsum(counts) - counts
    pcounts = ((counts + ROUTE_BLOCK - 1) // ROUTE_BLOCK) * ROUTE_BLOCK
    pends = jnp.cumsum(pcounts)
    pstarts = pends - pcounts
    dest = (pstarts[se] + (jnp.arange(tk) - starts[se])).astype(jnp.int32)
    nb = -(-tk // ROUTE_BLOCK) + N_EXPERTS
    p = nb * ROUTE_BLOCK
    row_tok = jnp.zeros((p,), jnp.int32).at[dest].set(flat_tok[order])
    row_w = jnp.zeros((p,), F32).at[dest].set(wts.reshape(-1)[order])
    block_e = jnp.minimum(jnp.searchsorted(pends, jnp.arange(nb) * ROUTE_BLOCK, side='right'),
                          N_EXPERTS - 1).astype(jnp.int32)
    pos = jnp.zeros((tk,), jnp.int32).at[order].set(dest)
    pos = pos.reshape(t // COMB_TOK, COMB_TOK, TOP_K).transpose(0, 2, 1).reshape(t // COMB_TOK, TOP_K * COMB_TOK)
    return block_e, row_tok, row_w, pos


def _rope_tables(n):
    rows = n // GRID_W
    row = jnp.repeat(jnp.arange(rows), GRID_W).astype(F32)
    col = jnp.tile(jnp.arange(GRID_W), rows).astype(F32)
    half = QK_ROPE // 2
    inv = 1.0 / (ROPE_BASE ** (jnp.arange(0, half, 2, dtype=F32) / half))
    ang = jnp.stack([row[:, None] * inv, col[:, None] * inv], axis=1)
    cos, sin = jnp.cos(ang), jnp.sin(ang)
    c32 = jnp.broadcast_to(cos[:, :, None, :], (n, 2, 2, half // 2)).reshape(n, QK_ROPE)
    s32 = jnp.broadcast_to(sin[:, :, None, :], (n, 2, 2, half // 2)).reshape(n, QK_ROPE)
    return c32, s32


def _swap_signed(w):
    wr = w.reshape(w.shape[:-1] + (2, 2, QK_ROPE // 4))
    return jnp.stack([-wr[..., 1, :], wr[..., 0, :]], axis=-2).reshape(w.shape)


def _head_group(parts, lead):
    width = sum(p.shape[-1] for p in parts)
    pad = jnp.zeros(lead + (N_HEADS, HEAD_LANES - width), F32)
    return jnp.concatenate(list(parts) + [pad], axis=-1).reshape(lead + (N_HEADS * HEAD_LANES,))


def kernel(x, c, ctx, c_ctx, ada_w, ada_b, norm_mix_w, norm_ffn_w, ev_w_in, ev_conv_w, ev_conv_b, ev_ln_w,
           ev_ln_b, ev_q_norm_w, ev_kv_norm_w, ev_w_uq, ev_w_ukv, ev_w_out, od_w_in, od_conv_w, od_w_out,
           router_w, router_bias, exp_w_gate, exp_w_up, exp_w_down, sh_w_gate, sh_w_up, sh_w_down,
           final_norm_w):
    bsz, n, d = x.shape
    nctx = ctx.shape[1]
    t = bsz * n
    tm = 512
    x2 = x.reshape(t, d)

    c16 = jnp.zeros((16, d), F32).at[0:bsz].set(c).at[bsz].set(c_ctx)
    mod = _ada(c16, ada_w, ada_b)
    mod0 = mod[0].reshape(16, 1, 6 * d)
    mod1 = mod[1].reshape(16, 1, 6 * d)

    w_in = ev_w_in[0]
    o_kr = 2 * CONV_CH + Q_LORA + KV_LORA
    w_kr = w_in[:, o_kr:o_kr + QK_ROPE]
    win_ext = jnp.concatenate(
        [w_in[:, :o_kr], w_kr, _swap_signed(w_kr), jnp.zeros((d, HEAD_LANES - 2 * QK_ROPE), F32)], axis=1).astype(BF16)
    wuq = ev_w_uq[0]
    wq_rope = wuq[..., QK_NOPE:]
    zq = jnp.zeros((Q_LORA, N_HEADS, QK_NOPE), F32)
    wq_ext = jnp.concatenate([_head_group([wuq], (Q_LORA,)),
                              _head_group([zq, _swap_signed(wq_rope)], (Q_LORA,))], axis=1).astype(BF16)
    wukv = ev_w_ukv[0]
    wkv_ext = jnp.concatenate([_head_group([wukv[..., :QK_NOPE]], (KV_LORA,)),
                               _head_group([wukv[..., QK_NOPE:]], (KV_LORA,))], axis=1).astype(BF16)
    eye = jnp.eye(QK_ROPE, dtype=F32)
    e_small = _head_group([jnp.zeros((QK_ROPE, N_HEADS, QK_NOPE), F32),
                           jnp.broadcast_to(eye[:, None, :], (QK_ROPE, N_HEADS, QK_ROPE))], (QK_ROPE,))
    emat = jnp.concatenate([e_small, jnp.zeros((HEAD_LANES - QK_ROPE, N_HEADS * HEAD_LANES), F32)], axis=0).astype(BF16)

    c32, s32 = _rope_tables(n)
    scale = (QK_NOPE + QK_ROPE) ** -0.5
    zpad = jnp.zeros((n, HEAD_LANES - 2 * QK_ROPE), F32)
    ck_lat = jnp.concatenate([c32, s32, zpad], axis=1)
    zrope = jnp.zeros((n, HEAD_LANES - QK_NOPE - QK_ROPE), F32)
    cq_lat = jnp.concatenate([jnp.full((n, QK_NOPE), scale, F32), c32 * scale, zrope,
                              jnp.zeros((n, QK_NOPE), F32), s32 * scale, zrope], axis=1)
    ck_ctx = jnp.concatenate([jnp.ones((nctx, QK_ROPE), F32), jnp.zeros((nctx, HEAD_LANES - QK_ROPE), F32)], axis=1)
    cq_ctx = jnp.zeros((nctx, 2 * HEAD_LANES), F32)

    nw0 = norm_mix_w[0].reshape(1, d)
    qnw = ev_q_norm_w[0].reshape(1, Q_LORA)
    kvnw = ev_kv_norm_w[0].reshape(1, KV_LORA)
    glu, q, k_l, v_l = _inproj0(x2, mod0, lambda i: (i * tm) // n, nw0, win_ext, qnw, kvnw, wq_ext, wkv_ext,
                                emat, ck_lat, cq_lat, latent=True, tm=tm, seq=n)
    tmc = min(tm, nctx)
    k_c, v_c = _inproj0(ctx.reshape(bsz * nctx, d), mod0, lambda i: bsz, nw0,
                        win_ext[:, 2 * CONV_CH + Q_LORA:], qnw, kvnw, wq_ext, wkv_ext,
                        emat, ck_ctx, cq_ctx, latent=False, tm=tmc, seq=nctx)

    o_a = _conv(glu, ev_conv_w[0].reshape(CONV_WIDTH, CONV_CH), ev_conv_b[0].reshape(1, CONV_CH),
                ev_ln_w[0].reshape(1, CONV_CH), ev_ln_b[0].reshape(1, CONV_CH), tn=tm, seq=n)
    nk = N_HEADS * HEAD_LANES
    o_b = _attn(q.reshape(bsz, n, nk), k_c.reshape(bsz, nctx, nk), k_l.reshape(bsz, n, nk),
                v_c.reshape(bsz, nctx, nk), v_l.reshape(bsz, n, nk), tq=256)
    o_b = o_b.reshape(t, N_HEADS * V_HEAD)

    def tail_weights(i):
        rw = jnp.concatenate([router_w[i], jnp.zeros((d, HEAD_LANES - N_EXPERTS), F32)], axis=1)
        rwh = rw.astype(BF16)
        rwl = (rw - rwh.astype(F32)).astype(BF16)
        return (norm_ffn_w[i].reshape(1, d), sh_w_gate[i].astype(BF16), sh_w_up[i].astype(BF16),
                sh_w_down[i].astype(BF16), rwh, rwl)

    def moe(i, f, scores):
        eidx, wts = _route(scores[:, :N_EXPERTS], router_bias[i])
        block_e, row_tok, row_w, pos = _dispatch_plan(eidx, wts)
        y = _experts(block_e, row_tok, row_w, f, exp_w_gate[i].astype(BF16), exp_w_up[i].astype(BF16),
                     exp_w_down[i].astype(BF16))
        return pos, y

    w_out = ev_w_out[0].astype(BF16)
    f0, base0, scores0 = _mix0(o_a, o_b, x2, mod0, w_out[:CONV_CH], w_out[CONV_CH:], tail_weights(0), tm=tm, seq=n)
    pos0, y0 = moe(0, f0, scores0)
    fnw = final_norm_w.reshape(1, d)
    h = _combine(pos0, y0, base0, mod0, fnw, seq=n, final=False)

    bq, z = _inproj1(h, mod1, norm_mix_w[1].reshape(1, d), od_w_in[0].astype(BF16), tm=tm, seq=n)
    f1, base1, scores1 = _mix1(bq, z, h, mod1, od_conv_w[0].reshape(3, d), od_w_out[0].astype(BF16),
                               tail_weights(1), tm=tm, seq=n)
    pos1, y1 = moe(1, f1, scores1)
    out = _combine(pos1, y1, base1, mod1, fnw, seq=n, final=True)
    return out.reshape(bsz, n, d)
```

```python
import functools

import jax
import jax.numpy as jnp
import numpy as np
from jax import lax
from jax.experimental import pallas as pl
from jax.experimental.pallas import tpu as pltpu

F32 = jnp.float32
BF16 = jnp.bfloat16

EPS = 1e-6
GRID_W = 64
CONV_CH = 512
CONV_WIDTH = 31
N_HEADS = 8
QK_NOPE = 64
QK_ROPE = 32
V_HEAD = 64
Q_LORA = 256
KV_LORA = 128
ROPE_BASE = 10000.0
N_EXPERTS = 64
TOP_K = 6
N_GROUPS = 8
TOPK_GROUPS = 4
ROUTED_SCALE = 2.5
ROUTE_BLOCK = 128

HEAD_LANES = 128
SEG_ROWS = 16
SORT_CHUNK = 512
VMEM_LIMIT = 56 * 1024 * 1024


def _cparams(*sem):
    return pltpu.CompilerParams(dimension_semantics=sem, vmem_limit_bytes=VMEM_LIMIT)


def _dot(a, b):
    return jnp.dot(a, b, preferred_element_type=F32)


def _dot_nt(a, b):
    return lax.dot_general(a, b, (((1,), (1,)), ((), ())), preferred_element_type=F32)


def _rms(x, w):
    return x * lax.rsqrt(jnp.mean(x * x, axis=-1, keepdims=True) + EPS) * w


def _silu(x):
    return x * jax.nn.sigmoid(x)


def _ada_kernel(c_ref, w_ref, b_ref, o_ref):
    c = c_ref[...]
    o_ref[0] = _dot(_silu(c), w_ref[0]) + b_ref[0]


def _ada(c16, ada_w, ada_b):
    depth, d, n6 = ada_w.shape
    tn = 1536
    return pl.pallas_call(
        _ada_kernel,
        grid=(depth, n6 // tn),
        in_specs=[
            pl.BlockSpec((16, d), lambda l, j: (0, 0)),
            pl.BlockSpec((1, d, tn), lambda l, j: (l, 0, j)),
            pl.BlockSpec((1, 1, tn), lambda l, j: (l, 0, j)),
        ],
        out_specs=pl.BlockSpec((1, 16, tn), lambda l, j: (l, 0, j)),
        out_shape=jax.ShapeDtypeStruct((depth, 16, n6), F32),
        compiler_params=_cparams("arbitrary", "arbitrary"),
        name="ada",
    )(c16, ada_w, ada_b.reshape(depth, 1, n6))


def _inproj0_kernel(x_ref, mod_ref, nw_ref, win_ref, qnw_ref, kvnw_ref, wq_ref, wkv_ref, e_ref,
                    ck_ref, cq_ref, *outs, latent, d):
    x = x_ref[...]
    mod = mod_ref[0]
    a = (_rms(x, nw_ref[...]) * (1.0 + mod[:, d:2 * d]) + mod[:, 0:d]).astype(BF16)
    u = _dot(a, win_ref[...])
    if latent:
        glu_ref, q_ref, k_ref, v_ref = outs
        glu_ref[...] = u[:, 0:CONV_CH] * jax.nn.sigmoid(u[:, CONV_CH:2 * CONV_CH])
        o = 2 * CONV_CH
        cqn = _rms(u[:, o:o + Q_LORA], qnw_ref[...]).astype(BF16)
        o += Q_LORA
        cosq = cq_ref[:, 0:HEAD_LANES]
        sinq = cq_ref[:, HEAD_LANES:2 * HEAD_LANES]
        nq = N_HEADS * HEAD_LANES
        qa = _dot(cqn, wq_ref[:, 0:nq])
        qb = _dot(cqn, wq_ref[:, nq:2 * nq])
        for h in range(N_HEADS):
            sl = slice(h * HEAD_LANES, (h + 1) * HEAD_LANES)
            q_ref[:, sl] = (qa[:, sl] * cosq + qb[:, sl] * sinq).astype(BF16)
    else:
        k_ref, v_ref = outs
        o = 0
    nk = N_HEADS * HEAD_LANES
    kvn = _rms(u[:, o:o + KV_LORA], kvnw_ref[...]).astype(BF16)
    o += KV_LORA
    prod = u[:, o:o + HEAD_LANES] * ck_ref[...]
    krot = (prod + pltpu.roll(prod, HEAD_LANES - QK_ROPE, 1)).astype(BF16)
    k = _dot(kvn, wkv_ref[:, 0:nk]) + _dot(krot, e_ref[...])
    k_ref[...] = k.astype(BF16)
    lane = lax.broadcasted_iota(jnp.int32, (1, nk), 1)
    ones = jnp.where(lane % HEAD_LANES == V_HEAD, 1.0, 0.0).astype(F32)
    v_ref[...] = (_dot(kvn, wkv_ref[:, nk:2 * nk]) + ones).astype(BF16)


def _inproj0(x2, mod3, mod_row_fn, nw, win, qnw, kvnw, wq, wkv, emat, ck, cq, *, latent, tm, seq):
    t, d = x2.shape
    nk = N_HEADS * HEAD_LANES
    tiles_per_seq = seq // tm
    full = lambda i: (0, 0)
    in_specs = [
        pl.BlockSpec((tm, d), lambda i: (i, 0)),
        pl.BlockSpec((1, 1, 6 * d), lambda i: (mod_row_fn(i), 0, 0)),
        pl.BlockSpec((1, d), full),
        pl.BlockSpec(win.shape, full),
        pl.BlockSpec((1, Q_LORA), full),
        pl.BlockSpec((1, KV_LORA), full),
        pl.BlockSpec(wq.shape, full),
        pl.BlockSpec(wkv.shape, full),
        pl.BlockSpec(emat.shape, full),
        pl.BlockSpec((tm, HEAD_LANES), lambda i: (i % tiles_per_seq, 0)),
        pl.BlockSpec((tm, 2 * HEAD_LANES), lambda i: (i % tiles_per_seq, 0)),
    ]
    row = lambda i: (i, 0)
    out_specs = [pl.BlockSpec((tm, nk), row), pl.BlockSpec((tm, nk), row)]
    out_shape = [jax.ShapeDtypeStruct((t, nk), BF16), jax.ShapeDtypeStruct((t, nk), BF16)]
    if latent:
        out_specs = [pl.BlockSpec((tm, CONV_CH), row), pl.BlockSpec((tm, nk), row)] + out_specs
        out_shape = [jax.ShapeDtypeStruct((t, CONV_CH), F32), jax.ShapeDtypeStruct((t, nk), BF16)] + out_shape
    return pl.pallas_call(
        functools.partial(_inproj0_kernel, latent=latent, d=d),
        grid=(t // tm,),
        in_specs=in_specs,
        out_specs=out_specs,
        out_shape=out_shape,
        compiler_params=_cparams("arbitrary"),
        name="inproj0_lat" if latent else "inproj0_ctx",
    )(x2, mod3, nw, win, qnw, kvnw, wq, wkv, emat, ck, cq)


CONV_HALO = 16
CONV_ROWS = 64


def _conv_kernel(main_ref, prev_ref, next_ref, w_ref, b_ref, lnw_ref, lnb_ref, o_ref, win_ref, *,
                 tn, tiles_per_seq):
    i = pl.program_id(0)
    first = (i % tiles_per_seq) == 0
    last = (i % tiles_per_seq) == tiles_per_seq - 1
    win_ref[0:CONV_HALO, :] = jnp.where(first, 0.0, prev_ref[...])
    win_ref[CONV_HALO:CONV_HALO + tn, :] = main_ref[...]
    win_ref[CONV_HALO + tn:, :] = jnp.where(last, 0.0, next_ref[...])
    pad = CONV_WIDTH // 2
    for c in range(tn // CONV_ROWS):
        base = c * CONV_ROWS + CONV_HALO - pad
        acc = win_ref[base:base + CONV_ROWS, :] * w_ref[0:1, :]
        for k in range(1, CONV_WIDTH):
            acc = acc + win_ref[base + k:base + k + CONV_ROWS, :] * w_ref[k:k + 1, :]
        y = acc + b_ref[...]
        mu = jnp.mean(y, axis=-1, keepdims=True)
        yc = y - mu
        var = jnp.mean(yc * yc, axis=-1, keepdims=True)
        yn = yc * lax.rsqrt(var + EPS) * lnw_ref[...] + lnb_ref[...]
        o_ref[c * CONV_ROWS:(c + 1) * CONV_ROWS, :] = _silu(yn).astype(BF16)


def _conv(glu, w, b, lnw, lnb, *, tn, seq):
    t, ch = glu.shape
    tiles_per_seq = seq // tn
    hb = tn // CONV_HALO
    nhb = t // CONV_HALO
    full = lambda i: (0, 0)
    return pl.pallas_call(
        functools.partial(_conv_kernel, tn=tn, tiles_per_seq=tiles_per_seq),
        grid=(t // tn,),
        in_specs=[
            pl.BlockSpec((tn, ch), lambda i: (i, 0)),
            pl.BlockSpec((CONV_HALO, ch), lambda i: (jnp.maximum(i * hb - 1, 0), 0)),
            pl.BlockSpec((CONV_HALO, ch), lambda i: (jnp.minimum((i + 1) * hb, nhb - 1), 0)),
            pl.BlockSpec(w.shape, full),
            pl.BlockSpec((1, ch), full),
            pl.BlockSpec((1, ch), full),
            pl.BlockSpec((1, ch), full),
        ],
        out_specs=pl.BlockSpec((tn, ch), lambda i: (i, 0)),
        out_shape=jax.ShapeDtypeStruct((t, ch), BF16),
        scratch_shapes=[pltpu.VMEM((tn + 2 * CONV_HALO, ch), F32)],
        compiler_params=_cparams("arbitrary"),
        name="conv",
    )(glu, glu, glu, w, b, lnw, lnb)


ATT_KC = 512


def _attn_kernel(q_ref, kc_ref, kl_ref, vc_ref, vl_ref, o_ref, s_ref, *, nctx, nlat):
    outs = []
    for hh in range(2):
        sl = slice(hh * HEAD_LANES, (hh + 1) * HEAD_LANES)
        q = q_ref[0, :, sl]
        s = _dot_nt(q, kc_ref[0, :, sl])
        s_ref[:, 0:nctx] = s
        m = jnp.max(s, axis=-1, keepdims=True)
        for c in range(nlat // ATT_KC):
            s = _dot_nt(q, kl_ref[0, c * ATT_KC:(c + 1) * ATT_KC, sl])
            s_ref[:, nctx + c * ATT_KC:nctx + (c + 1) * ATT_KC] = s
            m = jnp.maximum(m, jnp.max(s, axis=-1, keepdims=True))
        p = jnp.exp(s_ref[:, 0:nctx] - m).astype(BF16)
        acc = _dot(p, vc_ref[0, :, sl])
        for c in range(nlat // ATT_KC):
            p = jnp.exp(s_ref[:, nctx + c * ATT_KC:nctx + (c + 1) * ATT_KC] - m).astype(BF16)
            acc = acc + _dot(p, vl_ref[0, c * ATT_KC:(c + 1) * ATT_KC, sl])
        outs.append(acc[:, 0:V_HEAD] / acc[:, V_HEAD:V_HEAD + 1])
    o_ref[0] = jnp.concatenate(outs, axis=-1).astype(BF16)


def _attn(q, kc, kl, vc, vl, *, tq):
    b, n, _ = q.shape
    nctx = kc.shape[1]
    hp = N_HEADS // 2
    w2 = 2 * HEAD_LANES
    return pl.pallas_call(
        functools.partial(_attn_kernel, nctx=nctx, nlat=n),
        grid=(b, hp, n // tq),
        in_specs=[
            pl.BlockSpec((1, tq, w2), lambda bi, j, i: (bi, i, j)),
            pl.BlockSpec((1, nctx, w2), lambda bi, j, i: (bi, 0, j)),
            pl.BlockSpec((1, n, w2), lambda bi, j, i: (bi, 0, j)),
            pl.BlockSpec((1, nctx, w2), lambda bi, j, i: (bi, 0, j)),
            pl.BlockSpec((1, n, w2), lambda bi, j, i: (bi, 0, j)),
        ],
        out_specs=pl.BlockSpec((1, tq, 2 * V_HEAD), lambda bi, j, i: (bi, i, j)),
        out_shape=jax.ShapeDtypeStruct((b, n, N_HEADS * V_HEAD), BF16),
        scratch_shapes=[pltpu.VMEM((tq, nctx + n), F32)],
        compiler_params=_cparams("arbitrary", "arbitrary", "arbitrary"),
        name="attn",
    )(q, kc, kl, vc, vl)


def _route(lt, rb_ref, utri_ref, ltri_ref, ls_ref, wn_ref, seg_ref, tm):
    ninf = -jnp.inf
    gsz = N_EXPERTS // N_GROUPS
    s = jax.nn.sigmoid(lt)
    sel = s + rb_ref[...]
    sub = lax.broadcasted_iota(jnp.int32, (gsz, tm), 0).astype(F32)
    gs = []
    for g in range(N_GROUPS):
        xg = sel[g * gsz:(g + 1) * gsz]
        m1 = jnp.max(xg, axis=0, keepdims=True)
        i1 = jnp.min(jnp.where(xg == m1, sub, float(gsz)), axis=0, keepdims=True)
        m2 = jnp.max(jnp.where(sub == i1, ninf, xg), axis=0, keepdims=True)
        gs.append(m1 + m2)
    rows = []
    for g in range(N_GROUPS):
        beaten = jnp.zeros((1, tm), F32)
        for o in range(N_GROUPS):
            if o != g:
                beats = (gs[o] >= gs[g]) if o < g else (gs[o] > gs[g])
                beaten = beaten + jnp.where(beats, 1.0, 0.0)
        rows.append(jnp.where(beaten < float(TOPK_GROUPS), sel[g * gsz:(g + 1) * gsz], ninf))
    cur = jnp.concatenate(rows, axis=0)
    ei = lax.broadcasted_iota(jnp.int32, (N_EXPERTS, tm), 0).astype(F32)
    hits, ws = [], []
    for k in range(TOP_K):
        m = jnp.max(cur, axis=0, keepdims=True)
        idx = jnp.min(jnp.where(cur == m, ei, float(N_EXPERTS)), axis=0, keepdims=True)
        hit = ei == idx
        ws.append(jnp.sum(jnp.where(hit, s, 0.0), axis=0, keepdims=True))
        cur = jnp.where(hit, ninf, cur)
        hits.append(jnp.where(hit, 1.0, 0.0))
    wsum = ws[0]
    for k in range(1, TOP_K):
        wsum = wsum + ws[k]
    pre = _dot(jnp.concatenate(hits, axis=0).astype(BF16), utri_ref[...])
    tots = [jnp.sum(h, axis=1, keepdims=True) for h in hits]
    cnt = tots[0]
    for k in range(1, TOP_K):
        cnt = cnt + tots[k]
    seg = jnp.floor((cnt + (SEG_ROWS - 1.0)) * (1.0 / SEG_ROWS))
    segb = jnp.broadcast_to(seg, (N_EXPERTS, HEAD_LANES))
    seg_ref[0] = segb
    lstart = _dot(ltri_ref[...], segb.astype(BF16))[:, 0:1] * float(SEG_ROWS)
    basek = lstart
    for k in range(TOP_K):
        rank = jnp.sum(hits[k] * (pre[k * N_EXPERTS:(k + 1) * N_EXPERTS] + basek), axis=0, keepdims=True)
        ls_ref[k:k + 1, :] = rank.astype(jnp.int32)
        wn_ref[k:k + 1, :] = ws[k] / wsum * ROUTED_SCALE
        basek = basek + tots[k]
    ls_ref[TOP_K:8, :] = jnp.full((8 - TOP_K, tm), -1, jnp.int32)
    wn_ref[TOP_K:8, :] = jnp.zeros((8 - TOP_K, tm), F32)


def _tail(y, x, mod, d, tm, nfw_ref, swg_ref, swu_ref, swd_ref, rwh_ref, rwl_ref, rb_ref, utri_ref, ltri_ref,
          fb_ref, base_ref, ls_ref, wn_ref, seg_ref):
    g1 = mod[:, 2 * d:3 * d]
    sh2 = mod[:, 3 * d:4 * d]
    sc2 = mod[:, 4 * d:5 * d]
    g2 = mod[:, 5 * d:6 * d]
    h1 = x + g1 * y
    f = _rms(h1, nfw_ref[...]) * (1.0 + sc2) + sh2
    fb = f.astype(BF16)
    fb_ref[...] = fb
    act = (_silu(_dot(fb, swg_ref[...])) * _dot(fb, swu_ref[...])).astype(BF16)
    base_ref[...] = h1 + g2 * _dot(act, swd_ref[...])
    flo = (f - fb.astype(F32)).astype(BF16)
    logits = _dot(fb, rwh_ref[...]) + (_dot(flo, rwh_ref[...]) + _dot(fb, rwl_ref[...]))
    _route(logits.T[0:N_EXPERTS], rb_ref, utri_ref, ltri_ref, ls_ref, wn_ref, seg_ref, tm)


def _mix0_kernel(oa_ref, ob_ref, x_ref, mod_ref, woa_ref, wob_ref, *rest, d, tm):
    y = _dot(oa_ref[...], woa_ref[...]) + _dot(ob_ref[...], wob_ref[...])
    _tail(y, x_ref[...], mod_ref[0], d, tm, *rest)


def _mix1_kernel(b_ref, z_ref, zp_ref, zn_ref, x_ref, mod_ref, cw_ref, wo_ref, *rest, d, tm, tiles_per_seq):
    i = pl.program_id(0)
    first = (i % tiles_per_seq) == 0
    last = (i % tiles_per_seq) == tiles_per_seq - 1
    z = z_ref[...].astype(F32)
    rowi = lax.broadcasted_iota(jnp.int32, (tm, 1), 0)
    zprev_row = jnp.where(first, 0.0, zp_ref[7:8, :].astype(F32))
    znext_row = jnp.where(last, 0.0, zn_ref[0:1, :].astype(F32))
    zm = jnp.where(rowi == 0, zprev_row, pltpu.roll(z, 1, 0))
    zp = jnp.where(rowi == tm - 1, znext_row, pltpu.roll(z, tm - 1, 0))
    conv = zm * cw_ref[0:1, :] + z * cw_ref[1:2, :] + zp * cw_ref[2:3, :]
    v = (b_ref[...].astype(F32) * conv).astype(BF16)
    _tail(_dot(v, wo_ref[...]), x_ref[...], mod_ref[0], d, tm, *rest)


def _tail_specs(t, d, tm, tail_w):
    full = lambda i: (0, 0)
    row = lambda i: (i, 0)
    col = lambda i: (0, i)
    in_specs = [pl.BlockSpec(w.shape, full) for w in tail_w]
    out_specs = [pl.BlockSpec((tm, d), row), pl.BlockSpec((tm, d), row), pl.BlockSpec((8, tm), col),
                 pl.BlockSpec((8, tm), col), pl.BlockSpec((1, N_EXPERTS, HEAD_LANES), lambda i: (i, 0, 0))]
    out_shape = [jax.ShapeDtypeStruct((t, d), BF16), jax.ShapeDtypeStruct((t, d), F32),
                 jax.ShapeDtypeStruct((8, t), jnp.int32), jax.ShapeDtypeStruct((8, t), F32),
                 jax.ShapeDtypeStruct((t // tm, N_EXPERTS, HEAD_LANES), F32)]
    return in_specs, out_specs, out_shape


def _mix0(oa, ob, x2, mod3, woa, wob, tail_w, *, tm, seq):
    t, d = x2.shape
    full = lambda i: (0, 0)
    row = lambda i: (i, 0)
    tin, out_specs, out_shape = _tail_specs(t, d, tm, tail_w)
    return pl.pallas_call(
        functools.partial(_mix0_kernel, d=d, tm=tm),
        grid=(t // tm,),
        in_specs=[pl.BlockSpec((tm, oa.shape[1]), row), pl.BlockSpec((tm, ob.shape[1]), row),
                  pl.BlockSpec((tm, d), row),
                  pl.BlockSpec((1, 1, 6 * d), lambda i: ((i * tm) // seq, 0, 0)),
                  pl.BlockSpec(woa.shape, full), pl.BlockSpec(wob.shape, full)] + tin,
        out_specs=out_specs,
        out_shape=out_shape,
        compiler_params=_cparams("arbitrary"),
        name="mix0_tail",
    )(oa, ob, x2, mod3, woa, wob, *tail_w)


def _mix1(bq, z, x2, mod3, cw, wo, tail_w, *, tm, seq):
    t, d = x2.shape
    full = lambda i: (0, 0)
    row = lambda i: (i, 0)
    hb = tm // 8
    nhb = t // 8
    tin, out_specs, out_shape = _tail_specs(t, d, tm, tail_w)
    return pl.pallas_call(
        functools.partial(_mix1_kernel, d=d, tm=tm, tiles_per_seq=seq // tm),
        grid=(t // tm,),
        in_specs=[pl.BlockSpec((tm, d), row), pl.BlockSpec((tm, d), row),
                  pl.BlockSpec((8, d), lambda i: (jnp.maximum(i * hb - 1, 0), 0)),
                  pl.BlockSpec((8, d), lambda i: (jnp.minimum((i + 1) * hb, nhb - 1), 0)),
                  pl.BlockSpec((tm, d), row),
                  pl.BlockSpec((1, 1, 6 * d), lambda i: ((i * tm) // seq, 0, 0)),
                  pl.BlockSpec(cw.shape, full), pl.BlockSpec(wo.shape, full)] + tin,
        out_specs=out_specs,
        out_shape=out_shape,
        compiler_params=_cparams("arbitrary"),
        name="mix1_tail",
    )(bq, z, z, z, x2, mod3, cw, wo, *tail_w)


def _inproj1_kernel(x_ref, mod_ref, nw_ref, win_ref, b_ref, z_ref, *, d):
    mod = mod_ref[0]
    a = (_rms(x_ref[...], nw_ref[...]) * (1.0 + mod[:, d:2 * d]) + mod[:, 0:d]).astype(BF16)
    u = _dot(a, win_ref[...])
    b_ref[...] = u[:, 0:d].astype(BF16)
    z_ref[...] = (u[:, d:2 * d] * u[:, 2 * d:3 * d]).astype(F32)


def _inproj1(x2, mod3, nw, win, *, tm, seq):
    t, d = x2.shape
    full = lambda i: (0, 0)
    row = lambda i: (i, 0)
    return pl.pallas_call(
        functools.partial(_inproj1_kernel, d=d),
        grid=(t // tm,),
        in_specs=[pl.BlockSpec((tm, d), row),
                  pl.BlockSpec((1, 1, 6 * d), lambda i: ((i * tm) // seq, 0, 0)),
                  pl.BlockSpec((1, d), full), pl.BlockSpec(win.shape, full)],
        out_specs=[pl.BlockSpec((tm, d), row), pl.BlockSpec((tm, d), row)],
        out_shape=[jax.ShapeDtypeStruct((t, d), BF16), jax.ShapeDtypeStruct((t, d), F32)],
        compiler_params=_cparams("arbitrary"),
        name="inproj1",
    )(x2, mod3, nw, win)


def _seg_copies(seg_ref, lst_ref, goff_ref, j, local, slot, glob, sem, to_global):
    def ebody(e, carry):
        n = seg_ref[j * N_EXPERTS + e]
        s0 = lst_ref[j * N_EXPERTS + e]
        d0 = goff_ref[j * N_EXPERTS + e]

        def gbody(g, c2):
            lrow = pl.multiple_of((s0 + g) * SEG_ROWS, SEG_ROWS)
            grow = pl.multiple_of((d0 + g) * SEG_ROWS, SEG_ROWS)
            lref = local.at[slot, pl.ds(lrow, SEG_ROWS)]
            gref = glob.at[pl.ds(grow, SEG_ROWS)]
            if to_global:
                pltpu.make_async_copy(lref, gref, sem.at[slot]).start()
            else:
                pltpu.make_async_copy(gref, lref, sem.at[slot]).start()
            return c2
        lax.fori_loop(0, n, gbody, 0)
        return carry
    lax.fori_loop(0, N_EXPERTS, ebody, 0)


def _seg_wait(n, local, slot, glob, sem):
    def body(i, carry):
        pltpu.make_async_copy(glob.at[pl.ds(0, SEG_ROWS)], local.at[slot, pl.ds(0, SEG_ROWS)], sem.at[slot]).wait()
        return carry
    lax.fori_loop(0, n, body, 0)


def _dispatch_kernel(seg_ref, lst_ref, goff_ref, ntot_ref, gap_ref, ls_ref, f_ref, xs_hbm, loc, zbuf, sem, zsem, *,
                     tm, lrows):
    j = pl.program_id(0)
    nj = pl.num_programs(0)
    slot = j % 2

    @pl.when(j >= 2)
    def _():
        _seg_wait(ntot_ref[j - 2], loc, slot, xs_hbm, sem)

    nrows = ntot_ref[j] * SEG_ROWS
    for rc in range(lrows // SORT_CHUNK):
        @pl.when(rc * SORT_CHUNK < nrows)
        def _():
            riota = lax.broadcasted_iota(jnp.int32, (SORT_CHUNK, tm), 0) + rc * SORT_CHUNK
            p = jnp.where(riota == ls_ref[0:1, :], 1.0, 0.0)
            for k in range(1, TOP_K):
                p = jnp.where(riota == ls_ref[k:k + 1, :], 1.0, p)
            loc[slot, rc * SORT_CHUNK:(rc + 1) * SORT_CHUNK, :] = _dot(p.astype(BF16), f_ref[...]).astype(BF16)

    _seg_copies(seg_ref, lst_ref, goff_ref, j, loc, slot, xs_hbm, sem, True)

    @pl.when(j == nj - 1)
    def _():
        zbuf[...] = jnp.zeros(zbuf.shape, BF16)
        zgran = zbuf.at[pl.ds(0, SEG_ROWS)]

        def zbody(e, carry):
            def gbody(g, c2):
                grow = pl.multiple_of((gap_ref[e] + g) * SEG_ROWS, SEG_ROWS)
                pltpu.make_async_copy(zgran, xs_hbm.at[pl.ds(grow, SEG_ROWS)], zsem.at[0]).start()
                return c2
            lax.fori_loop(0, gap_ref[N_EXPERTS + e], gbody, 0)
            return carry
        lax.fori_loop(0, N_EXPERTS, zbody, 0)

        def tbody(b, carry):
            brow = pl.multiple_of(b * ROUTE_BLOCK, ROUTE_BLOCK)
            pltpu.make_async_copy(zbuf, xs_hbm.at[pl.ds(brow, ROUTE_BLOCK)], zsem.at[1]).start()
            return carry
        nblocks = xs_hbm.shape[0] // ROUTE_BLOCK
        lax.fori_loop(gap_ref[2 * N_EXPERTS + 1], nblocks, tbody, 0)

        def zwait(i, carry):
            pltpu.make_async_copy(zgran, xs_hbm.at[pl.ds(0, SEG_ROWS)], zsem.at[0]).wait()
            return carry
        lax.fori_loop(0, gap_ref[2 * N_EXPERTS], zwait, 0)

        def twait(b, carry):
            pltpu.make_async_copy(zbuf, xs_hbm.at[pl.ds(0, ROUTE_BLOCK)], zsem.at[1]).wait()
            return carry
        lax.fori_loop(gap_ref[2 * N_EXPERTS + 1], nblocks, twait, 0)
        _seg_wait(ntot_ref[j], loc, slot, xs_hbm, sem)

        @pl.when(j >= 1)
        def _():
            _seg_wait(ntot_ref[j - 1], loc, 1 - slot, xs_hbm, sem)


def _dispatch(plan, ls, fb, *, tm, lrows, prows):
    t, d = fb.shape
    grid_spec = pltpu.PrefetchScalarGridSpec(
        num_scalar_prefetch=5,
        grid=(t // tm,),
        in_specs=[pl.BlockSpec((8, tm), lambda j, *_: (0, j)), pl.BlockSpec((tm, d), lambda j, *_: (j, 0))],
        out_specs=pl.BlockSpec(memory_space=pl.ANY),
        scratch_shapes=[pltpu.VMEM((2, lrows, d), BF16), pltpu.VMEM((ROUTE_BLOCK, d), BF16),
                        pltpu.SemaphoreType.DMA((2,)), pltpu.SemaphoreType.DMA((2,))],
    )
    return pl.pallas_call(
        functools.partial(_dispatch_kernel, tm=tm, lrows=lrows),
        grid_spec=grid_spec,
        out_shape=jax.ShapeDtypeStruct((prows, d), BF16),
        compiler_params=_cparams("arbitrary"),
        name="dispatch",
    )(plan["seg"], plan["lst"], plan["goff"], plan["ntot"], plan["gap"], ls, fb)


def _experts_kernel(be_ref, nv_ref, xs_ref, wg_ref, wu_ref, wd_ref, y_ref):
    valid = pl.program_id(0) < nv_ref[0]

    @pl.when(valid)
    def _():
        x = xs_ref[...]
        act = (_silu(_dot(x, wg_ref[0])) * _dot(x, wu_ref[0])).astype(BF16)
        y_ref[...] = _dot(act, wd_ref[0]).astype(BF16)

    @pl.when(jnp.logical_not(valid))
    def _():
        y_ref[...] = jnp.zeros(y_ref.shape, BF16)


def _experts(block_e, nvalid, xs, wg, wu, wd):
    prows, d = xs.shape
    de = wg.shape[2]
    blk = lambda i, be, nv: (jnp.minimum(i, nv[0] - 1), 0)
    wsel = lambda i, be, nv: (be[i], 0, 0)
    grid_spec = pltpu.PrefetchScalarGridSpec(
        num_scalar_prefetch=2,
        grid=(prows // ROUTE_BLOCK,),
        in_specs=[pl.BlockSpec((ROUTE_BLOCK, d), blk), pl.BlockSpec((1, d, de), wsel),
                  pl.BlockSpec((1, d, de), wsel), pl.BlockSpec((1, de, d), wsel)],
        out_specs=pl.BlockSpec((ROUTE_BLOCK, d), lambda i, be, nv: (i, 0)),
    )
    return pl.pallas_call(
        _experts_kernel,
        grid_spec=grid_spec,
        out_shape=jax.ShapeDtypeStruct((prows, d), BF16),
        compiler_params=_cparams("arbitrary"),
        name="experts",
    )(block_e, nvalid, xs, wg, wu, wd)


def _combine_kernel(seg_ref, lst_ref, goff_ref, ntot_ref, lst_t_ref, wn_t_ref, base_ref, mod_ref, fnw_ref, ys_hbm,
                    o_ref, loc, lb, wb, acc, sem, *, tm, lrows, d, final):
    j = pl.program_id(0)
    nj = pl.num_programs(0)
    slot = j % 2

    @pl.when(j == 0)
    def _():
        loc[...] = jnp.zeros(loc.shape, BF16)
        _seg_copies(seg_ref, lst_ref, goff_ref, 0, loc, 0, ys_hbm, sem, False)

    @pl.when(j + 1 < nj)
    def _():
        _seg_copies(seg_ref, lst_ref, goff_ref, j + 1, loc, 1 - slot, ys_hbm, sem, False)

    for k in range(TOP_K):
        lb[k] = jnp.broadcast_to(lst_t_ref[:, k:k + 1], (tm, HEAD_LANES))
        wb[k] = jnp.broadcast_to(wn_t_ref[:, k:k + 1], (tm, HEAD_LANES))
    acc[...] = jnp.zeros(acc.shape, F32)
    _seg_wait(ntot_ref[j], loc, slot, ys_hbm, sem)

    nrows = ntot_ref[j] * SEG_ROWS
    for rc in range(lrows // SORT_CHUNK):
        @pl.when(rc * SORT_CHUNK < nrows)
        def _():
            pieces = []
            for c in range(SORT_CHUNK // HEAD_LANES):
                ci = (lax.broadcasted_iota(jnp.int32, (tm, HEAD_LANES), 1)
                      + (rc * SORT_CHUNK + c * HEAD_LANES)).astype(F32)
                pw = jnp.where(lb[0] == ci, wb[0], 0.0)
                for k in range(1, TOP_K):
                    pw = jnp.where(lb[k] == ci, wb[k], pw)
                pieces.append(pw.astype(BF16))
            acc[...] += _dot(jnp.concatenate(pieces, axis=1), loc[slot, rc * SORT_CHUNK:(rc + 1) * SORT_CHUNK, :])

    h = base_ref[...] + mod_ref[0][:, 5 * d:6 * d] * acc[...]
    if final:
        h = _rms(h, fnw_ref[...])
    o_ref[...] = h


def _combine(plan, lst_t, wn_t, ys, base, mod3, fnw, *, tm, lrows, seq, final):
    t, d = base.shape
    full = lambda j, *_: (0, 0)
    row = lambda j, *_: (j, 0)
    grid_spec = pltpu.PrefetchScalarGridSpec(
        num_scalar_prefetch=4,
        grid=(t // tm,),
        in_specs=[pl.BlockSpec((tm, 8), row), pl.BlockSpec((tm, 8), row), pl.BlockSpec((tm, d), row),
                  pl.BlockSpec((1, 1, 6 * d), lambda j, *_: ((j * tm) // seq, 0, 0)),
                  pl.BlockSpec((1, d), full), pl.BlockSpec(memory_space=pl.ANY)],
        out_specs=pl.BlockSpec((tm, d), row),
        scratch_shapes=[pltpu.VMEM((2, lrows, d), BF16), pltpu.VMEM((TOP_K, tm, HEAD_LANES), F32),
                        pltpu.VMEM((TOP_K, tm, HEAD_LANES), F32), pltpu.VMEM((tm, d), F32),
                        pltpu.SemaphoreType.DMA((2,))],
    )
    return pl.pallas_call(
        functools.partial(_combine_kernel, tm=tm, lrows=lrows, d=d, final=final),
        grid_spec=grid_spec,
        out_shape=jax.ShapeDtypeStruct((t, d), F32),
        compiler_params=_cparams("arbitrary"),
        name="combine_final" if final else "combine",
    )(plan["seg"], plan["lst"], plan["goff"], plan["ntot"], lst_t, wn_t, base, mod3, fnw, ys)


def _dispatch_plan(seg_out, nblocks):
    seg = seg_out[:, :, 0].astype(jnp.int32)
    gpb = ROUTE_BLOCK // SEG_ROWS
    lst = jnp.cumsum(seg, axis=1) - seg
    tile_off = jnp.cumsum(seg, axis=0) - seg
    tot = jnp.sum(seg, axis=0)
    region_blk = (tot + gpb - 1) // gpb
    gend_blk = jnp.cumsum(region_blk)
    gstart = (gend_blk - region_blk) * gpb
    goff = gstart[None, :] + tile_off
    gap_cnt = region_blk * gpb - tot
    gap = jnp.concatenate([gstart + tot, gap_cnt, jnp.sum(gap_cnt)[None], gend_blk[-1:]]).astype(jnp.int32)
    block_e = jnp.minimum(jnp.searchsorted(gend_blk, jnp.arange(nblocks), side='right'), N_EXPERTS - 1)
    plan = {"seg": seg.reshape(-1), "lst": lst.reshape(-1).astype(jnp.int32),
            "goff": goff.reshape(-1).astype(jnp.int32), "ntot": jnp.sum(seg, axis=1).astype(jnp.int32), "gap": gap}
    return plan, block_e.astype(jnp.int32), gend_blk[-1:].astype(jnp.int32)


def _rope_tables(n):
    rows = n // GRID_W
    row = jnp.repeat(jnp.arange(rows), GRID_W).astype(F32)
    col = jnp.tile(jnp.arange(GRID_W), rows).astype(F32)
    half = QK_ROPE // 2
    inv = 1.0 / (ROPE_BASE ** (jnp.arange(0, half, 2, dtype=F32) / half))
    ang = jnp.stack([row[:, None] * inv, col[:, None] * inv], axis=1)
    cos, sin = jnp.cos(ang), jnp.sin(ang)
    c32 = jnp.broadcast_to(cos[:, :, None, :], (n, 2, 2, half // 2)).reshape(n, QK_ROPE)
    s32 = jnp.broadcast_to(sin[:, :, None, :], (n, 2, 2, half // 2)).reshape(n, QK_ROPE)
    return c32, s32


def _swap_signed(w):
    wr = w.reshape(w.shape[:-1] + (2, 2, QK_ROPE // 4))
    return jnp.stack([-wr[..., 1, :], wr[..., 0, :]], axis=-2).reshape(w.shape)


def _head_group(parts, lead):
    width = sum(p.shape[-1] for p in parts)
    pad = jnp.zeros(lead + (N_HEADS, HEAD_LANES - width), F32)
    return jnp.concatenate(list(parts) + [pad], axis=-1).reshape(lead + (N_HEADS * HEAD_LANES,))


def kernel(x, c, ctx, c_ctx, ada_w, ada_b, norm_mix_w, norm_ffn_w, ev_w_in, ev_conv_w, ev_conv_b, ev_ln_w,
           ev_ln_b, ev_q_norm_w, ev_kv_norm_w, ev_w_uq, ev_w_ukv, ev_w_out, od_w_in, od_conv_w, od_w_out,
           router_w, router_bias, exp_w_gate, exp_w_up, exp_w_down, sh_w_gate, sh_w_up, sh_w_down,
           final_norm_w):
    bsz, n, d = x.shape
    nctx = ctx.shape[1]
    t = bsz * n
    tm = 512
    x2 = x.reshape(t, d)

    c16 = jnp.zeros((16, d), F32).at[0:bsz].set(c).at[bsz].set(c_ctx)
    mod = _ada(c16, ada_w, ada_b)
    mod0 = mod[0].reshape(16, 1, 6 * d)
    mod1 = mod[1].reshape(16, 1, 6 * d)

    w_in = ev_w_in[0]
    o_kr = 2 * CONV_CH + Q_LORA + KV_LORA
    w_kr = w_in[:, o_kr:o_kr + QK_ROPE]
    win_ext = jnp.concatenate(
        [w_in[:, :o_kr], w_kr, _swap_signed(w_kr), jnp.zeros((d, HEAD_LANES - 2 * QK_ROPE), F32)], axis=1).astype(BF16)
    wuq = ev_w_uq[0]
    wq_rope = wuq[..., QK_NOPE:]
    zq = jnp.zeros((Q_LORA, N_HEADS, QK_NOPE), F32)
    wq_ext = jnp.concatenate([_head_group([wuq], (Q_LORA,)),
                              _head_group([zq, _swap_signed(wq_rope)], (Q_LORA,))], axis=1).astype(BF16)
    wukv = ev_w_ukv[0]
    wkv_ext = jnp.concatenate([_head_group([wukv[..., :QK_NOPE]], (KV_LORA,)),
                               _head_group([wukv[..., QK_NOPE:]], (KV_LORA,))], axis=1).astype(BF16)
    eye = jnp.eye(QK_ROPE, dtype=F32)
    e_small = _head_group([jnp.zeros((QK_ROPE, N_HEADS, QK_NOPE), F32),
                           jnp.broadcast_to(eye[:, None, :], (QK_ROPE, N_HEADS, QK_ROPE))], (QK_ROPE,))
    emat = jnp.concatenate([e_small, jnp.zeros((HEAD_LANES - QK_ROPE, N_HEADS * HEAD_LANES), F32)], axis=0).astype(BF16)

    c32, s32 = _rope_tables(n)
    scale = (QK_NOPE + QK_ROPE) ** -0.5
    zpad = jnp.zeros((n, HEAD_LANES - 2 * QK_ROPE), F32)
    ck_lat = jnp.concatenate([c32, s32, zpad], axis=1)
    zrope = jnp.zeros((n, HEAD_LANES - QK_NOPE - QK_ROPE), F32)
    cq_lat = jnp.concatenate([jnp.full((n, QK_NOPE), scale, F32), c32 * scale, zrope,
                              jnp.zeros((n, QK_NOPE), F32), s32 * scale, zrope], axis=1)
    ck_ctx = jnp.concatenate([jnp.ones((nctx, QK_ROPE), F32), jnp.zeros((nctx, HEAD_LANES - QK_ROPE), F32)], axis=1)
    cq_ctx = jnp.zeros((nctx, 2 * HEAD_LANES), F32)

    nw0 = norm_mix_w[0].reshape(1, d)
    qnw = ev_q_norm_w[0].reshape(1, Q_LORA)
    kvnw = ev_kv_norm_w[0].reshape(1, KV_LORA)
    glu, q, k_l, v_l = _inproj0(x2, mod0, lambda i: (i * tm) // n, nw0, win_ext, qnw, kvnw, wq_ext, wkv_ext,
                                emat, ck_lat, cq_lat, latent=True, tm=tm, seq=n)
    tmc = min(tm, nctx)
    k_c, v_c = _inproj0(ctx.reshape(bsz * nctx, d), mod0, lambda i: bsz, nw0,
                        win_ext[:, 2 * CONV_CH + Q_LORA:], qnw, kvnw, wq_ext, wkv_ext,
                        emat, ck_ctx, cq_ctx, latent=False, tm=tmc, seq=nctx)

    o_a = _conv(glu, ev_conv_w[0].reshape(CONV_WIDTH, CONV_CH), ev_conv_b[0].reshape(1, CONV_CH),
                ev_ln_w[0].reshape(1, CONV_CH), ev_ln_b[0].reshape(1, CONV_CH), tn=tm, seq=n)
    nk = N_HEADS * HEAD_LANES
    o_b = _attn(q.reshape(bsz, n, nk), k_c.reshape(bsz, nctx, nk), k_l.reshape(bsz, n, nk),
                v_c.reshape(bsz, nctx, nk), v_l.reshape(bsz, n, nk), tq=256)
    o_b = o_b.reshape(t, N_HEADS * V_HEAD)

    utri = (jnp.arange(tm)[:, None] < jnp.arange(tm)[None, :]).astype(BF16)
    ltri = (jnp.arange(N_EXPERTS)[None, :] < jnp.arange(N_EXPERTS)[:, None]).astype(BF16)

    def tail_weights(i):
        rw = jnp.concatenate([router_w[i], jnp.zeros((d, HEAD_LANES - N_EXPERTS), F32)], axis=1)
        rwh = rw.astype(BF16)
        rwl = (rw - rwh.astype(F32)).astype(BF16)
        return (norm_ffn_w[i].reshape(1, d), sh_w_gate[i].astype(BF16), sh_w_up[i].astype(BF16),
                sh_w_down[i].astype(BF16), rwh, rwl, router_bias[i].astype(F32).reshape(N_EXPERTS, 1), utri, ltri)

    ntiles = t // tm
    max_pad = SEG_ROWS - 1
    lrows = -(-(TOP_K * tm + N_EXPERTS * max_pad) // SORT_CHUNK) * SORT_CHUNK
    nblocks = -(-(TOP_K * t + ntiles * N_EXPERTS * max_pad + N_EXPERTS * (ROUTE_BLOCK - SEG_ROWS)) // ROUTE_BLOCK)
    fnw = final_norm_w.reshape(1, d)

    def moe(i, fb, base, ls, wn, seg_out, modi, final):
        plan, block_e, nvalid = _dispatch_plan(seg_out, nblocks)
        xs = _dispatch(plan, ls, fb, tm=tm, lrows=lrows, prows=nblocks * ROUTE_BLOCK)
        ys = _experts(block_e, nvalid, xs, exp_w_gate[i].astype(BF16), exp_w_up[i].astype(BF16),
                      exp_w_down[i].astype(BF16))
        return _combine(plan, ls.T.astype(F32), wn.T, ys, base, modi, fnw, tm=tm, lrows=lrows, seq=n, final=final)

    w_out = ev_w_out[0].astype(BF16)
    tail0 = _mix0(o_a, o_b, x2, mod0, w_out[:CONV_CH], w_out[CONV_CH:], tail_weights(0), tm=tm, seq=n)
    h = moe(0, *tail0, mod0, False)

    bq, z = _inproj1(h, mod1, norm_mix_w[1].reshape(1, d), od_w_in[0].astype(BF16), tm=tm, seq=n)
    tail1 = _mix1(bq, z, h, mod1, od_conv_w[0].reshape(3, d), od_w_out[0].astype(BF16), tail_weights(1), tm=tm, seq=n)
    out = moe(1, *tail1, mod1, True)
    return out.reshape(bsz, n, d)
```

```python
import functools

import jax
import jax.numpy as jnp
import numpy as np
from jax import lax
from jax.experimental import pallas as pl
from jax.experimental.pallas import tpu as pltpu

F32 = jnp.float32
BF16 = jnp.bfloat16

EPS = 1e-6
GRID_W = 64
CONV_CH = 512
CONV_WIDTH = 31
N_HEADS = 8
QK_NOPE = 64
QK_ROPE = 32
V_HEAD = 64
Q_LORA = 256
KV_LORA = 128
ROPE_BASE = 10000.0
N_EXPERTS = 64
TOP_K = 6
N_GROUPS = 8
TOPK_GROUPS = 4
ROUTED_SCALE = 2.5
ROUTE_BLOCK = 256

HEAD_LANES = 128
SEG_ROWS = 16
SORT_CHUNK = 512
VMEM_LIMIT = 56 * 1024 * 1024


def _cparams(*sem):
    return pltpu.CompilerParams(dimension_semantics=sem, vmem_limit_bytes=VMEM_LIMIT)


def _dot(a, b):
    return jnp.dot(a, b, preferred_element_type=F32)


def _dot_nt(a, b):
    return lax.dot_general(a, b, (((1,), (1,)), ((), ())), preferred_element_type=F32)


def _rms(x, w):
    return x * lax.rsqrt(jnp.mean(x * x, axis=-1, keepdims=True) + EPS) * w


def _silu(x):
    return x * jax.nn.sigmoid(x)


def _ada_kernel(c_ref, w_ref, b_ref, o_ref):
    c = c_ref[...]
    o_ref[0] = _dot(_silu(c), w_ref[0]) + b_ref[0]


def _ada(c16, ada_w, ada_b):
    depth, d, n6 = ada_w.shape
    tn = 1536
    return pl.pallas_call(
        _ada_kernel,
        grid=(depth, n6 // tn),
        in_specs=[
            pl.BlockSpec((16, d), lambda l, j: (0, 0)),
            pl.BlockSpec((1, d, tn), lambda l, j: (l, 0, j)),
            pl.BlockSpec((1, 1, tn), lambda l, j: (l, 0, j)),
        ],
        out_specs=pl.BlockSpec((1, 16, tn), lambda l, j: (l, 0, j)),
        out_shape=jax.ShapeDtypeStruct((depth, 16, n6), F32),
        compiler_params=_cparams("arbitrary", "arbitrary"),
        name="ada",
    )(c16, ada_w, ada_b.reshape(depth, 1, n6))


def _inproj0_kernel(x_ref, mod_ref, nw_ref, win_ref, qnw_ref, kvnw_ref, wq_ref, wkv_ref, e_ref,
                    ck_ref, cq_ref, *outs, latent, d):
    x = x_ref[...]
    mod = mod_ref[0]
    a = (_rms(x, nw_ref[...]) * (1.0 + mod[:, d:2 * d]) + mod[:, 0:d]).astype(BF16)
    u = _dot(a, win_ref[...])
    if latent:
        glu_ref, q_ref, k_ref, v_ref = outs
        glu_ref[...] = u[:, 0:CONV_CH] * jax.nn.sigmoid(u[:, CONV_CH:2 * CONV_CH])
        o = 2 * CONV_CH
        cqn = _rms(u[:, o:o + Q_LORA], qnw_ref[...]).astype(BF16)
        o += Q_LORA
        cosq = cq_ref[:, 0:HEAD_LANES]
        sinq = cq_ref[:, HEAD_LANES:2 * HEAD_LANES]
        nq = N_HEADS * HEAD_LANES
        qa = _dot(cqn, wq_ref[:, 0:nq])
        qb = _dot(cqn, wq_ref[:, nq:2 * nq])
        for h in range(N_HEADS):
            sl = slice(h * HEAD_LANES, (h + 1) * HEAD_LANES)
            q_ref[:, sl] = (qa[:, sl] * cosq + qb[:, sl] * sinq).astype(BF16)
    else:
        k_ref, v_ref = outs
        o = 0
    nk = N_HEADS * HEAD_LANES
    kvn = _rms(u[:, o:o + KV_LORA], kvnw_ref[...]).astype(BF16)
    o += KV_LORA
    prod = u[:, o:o + HEAD_LANES] * ck_ref[...]
    krot = (prod + pltpu.roll(prod, HEAD_LANES - QK_ROPE, 1)).astype(BF16)
    k = _dot(kvn, wkv_ref[:, 0:nk]) + _dot(krot, e_ref[...])
    k_ref[...] = k.astype(BF16)
    lane = lax.broadcasted_iota(jnp.int32, (1, nk), 1)
    ones = jnp.where(lane % HEAD_LANES == V_HEAD, 1.0, 0.0).astype(F32)
    v_ref[...] = (_dot(kvn, wkv_ref[:, nk:2 * nk]) + ones).astype(BF16)


def _inproj0(x2, mod3, mod_row_fn, nw, win, qnw, kvnw, wq, wkv, emat, ck, cq, *, latent, tm, seq):
    t, d = x2.shape
    nk = N_HEADS * HEAD_LANES
    tiles_per_seq = seq // tm
    full = lambda i: (0, 0)
    in_specs = [
        pl.BlockSpec((tm, d), lambda i: (i, 0)),
        pl.BlockSpec((1, 1, 6 * d), lambda i: (mod_row_fn(i), 0, 0)),
        pl.BlockSpec((1, d), full),
        pl.BlockSpec(win.shape, full),
        pl.BlockSpec((1, Q_LORA), full),
        pl.BlockSpec((1, KV_LORA), full),
        pl.BlockSpec(wq.shape, full),
        pl.BlockSpec(wkv.shape, full),
        pl.BlockSpec(emat.shape, full),
        pl.BlockSpec((tm, HEAD_LANES), lambda i: (i % tiles_per_seq, 0)),
        pl.BlockSpec((tm, 2 * HEAD_LANES), lambda i: (i % tiles_per_seq, 0)),
    ]
    row = lambda i: (i, 0)
    out_specs = [pl.BlockSpec((tm, nk), row), pl.BlockSpec((tm, nk), row)]
    out_shape = [jax.ShapeDtypeStruct((t, nk), BF16), jax.ShapeDtypeStruct((t, nk), BF16)]
    if latent:
        out_specs = [pl.BlockSpec((tm, CONV_CH), row), pl.BlockSpec((tm, nk), row)] + out_specs
        out_shape = [jax.ShapeDtypeStruct((t, CONV_CH), F32), jax.ShapeDtypeStruct((t, nk), BF16)] + out_shape
    return pl.pallas_call(
        functools.partial(_inproj0_kernel, latent=latent, d=d),
        grid=(t // tm,),
        in_specs=in_specs,
        out_specs=out_specs,
        out_shape=out_shape,
        compiler_params=_cparams("arbitrary"),
        name="inproj0_lat" if latent else "inproj0_ctx",
    )(x2, mod3, nw, win, qnw, kvnw, wq, wkv, emat, ck, cq)


CONV_HALO = 16
CONV_ROWS = 64


def _conv_kernel(main_ref, prev_ref, next_ref, w_ref, b_ref, lnw_ref, lnb_ref, o_ref, win_ref, *,
                 tn, tiles_per_seq):
    i = pl.program_id(0)
    first = (i % tiles_per_seq) == 0
    last = (i % tiles_per_seq) == tiles_per_seq - 1
    win_ref[0:CONV_HALO, :] = jnp.where(first, 0.0, prev_ref[...])
    win_ref[CONV_HALO:CONV_HALO + tn, :] = main_ref[...]
    win_ref[CONV_HALO + tn:, :] = jnp.where(last, 0.0, next_ref[...])
    pad = CONV_WIDTH // 2
    for c in range(tn // CONV_ROWS):
        base = c * CONV_ROWS + CONV_HALO - pad
        acc = win_ref[base:base + CONV_ROWS, :] * w_ref[0:1, :]
        for k in range(1, CONV_WIDTH):
            acc = acc + win_ref[base + k:base + k + CONV_ROWS, :] * w_ref[k:k + 1, :]
        y = acc + b_ref[...]
        mu = jnp.mean(y, axis=-1, keepdims=True)
        yc = y - mu
        var = jnp.mean(yc * yc, axis=-1, keepdims=True)
        yn = yc * lax.rsqrt(var + EPS) * lnw_ref[...] + lnb_ref[...]
        o_ref[c * CONV_ROWS:(c + 1) * CONV_ROWS, :] = _silu(yn).astype(BF16)


def _conv(glu, w, b, lnw, lnb, *, tn, seq):
    t, ch = glu.shape
    tiles_per_seq = seq // tn
    hb = tn // CONV_HALO
    nhb = t // CONV_HALO
    full = lambda i: (0, 0)
    return pl.pallas_call(
        functools.partial(_conv_kernel, tn=tn, tiles_per_seq=tiles_per_seq),
        grid=(t // tn,),
        in_specs=[
            pl.BlockSpec((tn, ch), lambda i: (i, 0)),
            pl.BlockSpec((CONV_HALO, ch), lambda i: (jnp.maximum(i * hb - 1, 0), 0)),
            pl.BlockSpec((CONV_HALO, ch), lambda i: (jnp.minimum((i + 1) * hb, nhb - 1), 0)),
            pl.BlockSpec(w.shape, full),
            pl.BlockSpec((1, ch), full),
            pl.BlockSpec((1, ch), full),
            pl.BlockSpec((1, ch), full),
        ],
        out_specs=pl.BlockSpec((tn, ch), lambda i: (i, 0)),
        out_shape=jax.ShapeDtypeStruct((t, ch), BF16),
        scratch_shapes=[pltpu.VMEM((tn + 2 * CONV_HALO, ch), F32)],
        compiler_params=_cparams("arbitrary"),
        name="conv",
    )(glu, glu, glu, w, b, lnw, lnb)


ATT_KC = 512


def _attn_kernel(q_ref, kc_ref, kl_ref, vc_ref, vl_ref, o_ref, s_ref, *, nctx, nlat):
    outs = []
    for hh in range(2):
        sl = slice(hh * HEAD_LANES, (hh + 1) * HEAD_LANES)
        q = q_ref[0, :, sl]
        s = _dot_nt(q, kc_ref[0, :, sl])
        s_ref[:, 0:nctx] = s
        m = jnp.max(s, axis=-1, keepdims=True)
        for c in range(nlat // ATT_KC):
            s = _dot_nt(q, kl_ref[0, c * ATT_KC:(c + 1) * ATT_KC, sl])
            s_ref[:, nctx + c * ATT_KC:nctx + (c + 1) * ATT_KC] = s
            m = jnp.maximum(m, jnp.max(s, axis=-1, keepdims=True))
        p = jnp.exp2(s_ref[:, 0:nctx] - m).astype(BF16)
        acc = _dot(p, vc_ref[0, :, sl])
        for c in range(nlat // ATT_KC):
            p = jnp.exp2(s_ref[:, nctx + c * ATT_KC:nctx + (c + 1) * ATT_KC] - m).astype(BF16)
            acc = acc + _dot(p, vl_ref[0, c * ATT_KC:(c + 1) * ATT_KC, sl])
        outs.append(acc[:, 0:V_HEAD] / acc[:, V_HEAD:V_HEAD + 1])
    o_ref[0] = jnp.concatenate(outs, axis=-1).astype(BF16)


def _attn(q, kc, kl, vc, vl, *, tq):
    b, n, _ = q.shape
    nctx = kc.shape[1]
    hp = N_HEADS // 2
    w2 = 2 * HEAD_LANES
    return pl.pallas_call(
        functools.partial(_attn_kernel, nctx=nctx, nlat=n),
        grid=(b, hp, n // tq),
        in_specs=[
            pl.BlockSpec((1, tq, w2), lambda bi, j, i: (bi, i, j)),
            pl.BlockSpec((1, nctx, w2), lambda bi, j, i: (bi, 0, j)),
            pl.BlockSpec((1, n, w2), lambda bi, j, i: (bi, 0, j)),
            pl.BlockSpec((1, nctx, w2), lambda bi, j, i: (bi, 0, j)),
            pl.BlockSpec((1, n, w2), lambda bi, j, i: (bi, 0, j)),
        ],
        out_specs=pl.BlockSpec((1, tq, 2 * V_HEAD), lambda bi, j, i: (bi, i, j)),
        out_shape=jax.ShapeDtypeStruct((b, n, N_HEADS * V_HEAD), BF16),
        scratch_shapes=[pltpu.VMEM((tq, nctx + n), F32)],
        compiler_params=_cparams("arbitrary", "arbitrary", "arbitrary"),
        name="attn",
    )(q, kc, kl, vc, vl)


def _route(lt, rb_ref, utri_ref, ltri_ref, ls_ref, wn_ref, seg_ref, tm):
    ninf = -jnp.inf
    gsz = N_EXPERTS // N_GROUPS
    s = jax.nn.sigmoid(lt)
    sel = s + rb_ref[...]
    sub = lax.broadcasted_iota(jnp.int32, (gsz, tm), 0).astype(F32)
    gs = []
    for g in range(N_GROUPS):
        xg = sel[g * gsz:(g + 1) * gsz]
        m1 = jnp.max(xg, axis=0, keepdims=True)
        i1 = jnp.min(jnp.where(xg == m1, sub, float(gsz)), axis=0, keepdims=True)
        m2 = jnp.max(jnp.where(sub == i1, ninf, xg), axis=0, keepdims=True)
        gs.append(m1 + m2)
    rows = []
    for g in range(N_GROUPS):
        beaten = jnp.zeros((1, tm), F32)
        for o in range(N_GROUPS):
            if o != g:
                beats = (gs[o] >= gs[g]) if o < g else (gs[o] > gs[g])
                beaten = beaten + jnp.where(beats, 1.0, 0.0)
        rows.append(jnp.where(beaten < float(TOPK_GROUPS), sel[g * gsz:(g + 1) * gsz], ninf))
    cur = jnp.concatenate(rows, axis=0)
    ei = lax.broadcasted_iota(jnp.int32, (N_EXPERTS, tm), 0).astype(F32)
    hits, ws = [], []
    for k in range(TOP_K):
        m = jnp.max(cur, axis=0, keepdims=True)
        idx = jnp.min(jnp.where(cur == m, ei, float(N_EXPERTS)), axis=0, keepdims=True)
        hit = ei == idx
        ws.append(jnp.sum(jnp.where(hit, s, 0.0), axis=0, keepdims=True))
        cur = jnp.where(hit, ninf, cur)
        hits.append(jnp.where(hit, 1.0, 0.0))
    wsum = ws[0]
    for k in range(1, TOP_K):
        wsum = wsum + ws[k]
    pre = _dot(jnp.concatenate(hits, axis=0).astype(BF16), utri_ref[...])
    tots = [jnp.sum(h, axis=1, keepdims=True) for h in hits]
    cnt = tots[0]
    for k in range(1, TOP_K):
        cnt = cnt + tots[k]
    seg = jnp.floor((cnt + (SEG_ROWS - 1.0)) * (1.0 / SEG_ROWS))
    segb = jnp.broadcast_to(seg, (N_EXPERTS, HEAD_LANES))
    seg_ref[0] = segb
    lstart = _dot(ltri_ref[...], segb.astype(BF16))[:, 0:1] * float(SEG_ROWS)
    basek = lstart
    for k in range(TOP_K):
        rank = jnp.sum(hits[k] * (pre[k * N_EXPERTS:(k + 1) * N_EXPERTS] + basek), axis=0, keepdims=True)
        ls_ref[k:k + 1, :] = rank.astype(jnp.int32)
        wn_ref[k:k + 1, :] = ws[k] / wsum * ROUTED_SCALE
        basek = basek + tots[k]
    ls_ref[TOP_K:8, :] = jnp.full((8 - TOP_K, tm), -1, jnp.int32)
    wn_ref[TOP_K:8, :] = jnp.zeros((8 - TOP_K, tm), F32)


def _tail(y, x, mod, d, tm, nfw_ref, swg_ref, swu_ref, swd_ref, rwh_ref, rwl_ref, rb_ref, utri_ref, ltri_ref,
          fb_ref, base_ref, ls_ref, wn_ref, seg_ref):
    g1 = mod[:, 2 * d:3 * d]
    sh2 = mod[:, 3 * d:4 * d]
    sc2 = mod[:, 4 * d:5 * d]
    g2 = mod[:, 5 * d:6 * d]
    h1 = x + g1 * y
    f = _rms(h1, nfw_ref[...]) * (1.0 + sc2) + sh2
    fb = f.astype(BF16)
    fb_ref[...] = fb
    act = (_silu(_dot(fb, swg_ref[...])) * _dot(fb, swu_ref[...])).astype(BF16)
    base_ref[...] = h1 + g2 * _dot(act, swd_ref[...])
    flo = (f - fb.astype(F32)).astype(BF16)
    logits = _dot(fb, rwh_ref[...]) + (_dot(flo, rwh_ref[...]) + _dot(fb, rwl_ref[...]))
    _route(logits.T[0:N_EXPERTS], rb_ref, utri_ref, ltri_ref, ls_ref, wn_ref, seg_ref, tm)


def _mix0_kernel(oa_ref, ob_ref, x_ref, mod_ref, woa_ref, wob_ref, *rest, d, tm):
    y = _dot(oa_ref[...], woa_ref[...]) + _dot(ob_ref[...], wob_ref[...])
    _tail(y, x_ref[...], mod_ref[0], d, tm, *rest)


def _mix1_kernel(b_ref, z_ref, zp_ref, zn_ref, x_ref, mod_ref, cw_ref, wo_ref, *rest, d, tm, tiles_per_seq):
    i = pl.program_id(0)
    first = (i % tiles_per_seq) == 0
    last = (i % tiles_per_seq) == tiles_per_seq - 1
    z = z_ref[...].astype(F32)
    rowi = lax.broadcasted_iota(jnp.int32, (tm, 1), 0)
    zprev_row = jnp.where(first, 0.0, zp_ref[7:8, :].astype(F32))
    znext_row = jnp.where(last, 0.0, zn_ref[0:1, :].astype(F32))
    zm = jnp.where(rowi == 0, zprev_row, pltpu.roll(z, 1, 0))
    zp = jnp.where(rowi == tm - 1, znext_row, pltpu.roll(z, tm - 1, 0))
    conv = zm * cw_ref[0:1, :] + z * cw_ref[1:2, :] + zp * cw_ref[2:3, :]
    v = (b_ref[...].astype(F32) * conv).astype(BF16)
    _tail(_dot(v, wo_ref[...]), x_ref[...], mod_ref[0], d, tm, *rest)


def _tail_specs(t, d, tm, tail_w):
    full = lambda i: (0, 0)
    row = lambda i: (i, 0)
    col = lambda i: (0, i)
    in_specs = [pl.BlockSpec(w.shape, full) for w in tail_w]
    out_specs = [pl.BlockSpec((tm, d), row), pl.BlockSpec((tm, d), row), pl.BlockSpec((8, tm), col),
                 pl.BlockSpec((8, tm), col), pl.BlockSpec((1, N_EXPERTS, HEAD_LANES), lambda i: (i, 0, 0))]
    out_shape = [jax.ShapeDtypeStruct((t, d), BF16), jax.ShapeDtypeStruct((t, d), F32),
                 jax.ShapeDtypeStruct((8, t), jnp.int32), jax.ShapeDtypeStruct((8, t), F32),
                 jax.ShapeDtypeStruct((t // tm, N_EXPERTS, HEAD_LANES), F32)]
    return in_specs, out_specs, out_shape


def _mix0(oa, ob, x2, mod3, woa, wob, tail_w, *, tm, seq):
    t, d = x2.shape
    full = lambda i: (0, 0)
    row = lambda i: (i, 0)
    tin, out_specs, out_shape = _tail_specs(t, d, tm, tail_w)
    return pl.pallas_call(
        functools.partial(_mix0_kernel, d=d, tm=tm),
        grid=(t // tm,),
        in_specs=[pl.BlockSpec((tm, oa.shape[1]), row), pl.BlockSpec((tm, ob.shape[1]), row),
                  pl.BlockSpec((tm, d), row),
                  pl.BlockSpec((1, 1, 6 * d), lambda i: ((i * tm) // seq, 0, 0)),
                  pl.BlockSpec(woa.shape, full), pl.BlockSpec(wob.shape, full)] + tin,
        out_specs=out_specs,
        out_shape=out_shape,
        compiler_params=_cparams("arbitrary"),
        name="mix0_tail",
    )(oa, ob, x2, mod3, woa, wob, *tail_w)


def _mix1(bq, z, x2, mod3, cw, wo, tail_w, *, tm, seq):
    t, d = x2.shape
    full = lambda i: (0, 0)
    row = lambda i: (i, 0)
    hb = tm // 8
    nhb = t // 8
    tin, out_specs, out_shape = _tail_specs(t, d, tm, tail_w)
    return pl.pallas_call(
        functools.partial(_mix1_kernel, d=d, tm=tm, tiles_per_seq=seq // tm),
        grid=(t // tm,),
        in_specs=[pl.BlockSpec((tm, d), row), pl.BlockSpec((tm, d), row),
                  pl.BlockSpec((8, d), lambda i: (jnp.maximum(i * hb - 1, 0), 0)),
                  pl.BlockSpec((8, d), lambda i: (jnp.minimum((i + 1) * hb, nhb - 1), 0)),
                  pl.BlockSpec((tm, d), row),
                  pl.BlockSpec((1, 1, 6 * d), lambda i: ((i * tm) // seq, 0, 0)),
                  pl.BlockSpec(cw.shape, full), pl.BlockSpec(wo.shape, full)] + tin,
        out_specs=out_specs,
        out_shape=out_shape,
        compiler_params=_cparams("arbitrary"),
        name="mix1_tail",
    )(bq, z, z, z, x2, mod3, cw, wo, *tail_w)


def _inproj1_kernel(x_ref, mod_ref, nw_ref, win_ref, b_ref, z_ref, *, d):
    mod = mod_ref[0]
    a = (_rms(x_ref[...], nw_ref[...]) * (1.0 + mod[:, d:2 * d]) + mod[:, 0:d]).astype(BF16)
    u = _dot(a, win_ref[...])
    b_ref[...] = u[:, 0:d].astype(BF16)
    z_ref[...] = (u[:, d:2 * d] * u[:, 2 * d:3 * d]).astype(F32)


def _inproj1(x2, mod3, nw, win, *, tm, seq):
    t, d = x2.shape
    full = lambda i: (0, 0)
    row = lambda i: (i, 0)
    return pl.pallas_call(
        functools.partial(_inproj1_kernel, d=d),
        grid=(t // tm,),
        in_specs=[pl.BlockSpec((tm, d), row),
                  pl.BlockSpec((1, 1, 6 * d), lambda i: ((i * tm) // seq, 0, 0)),
                  pl.BlockSpec((1, d), full), pl.BlockSpec(win.shape, full)],
        out_specs=[pl.BlockSpec((tm, d), row), pl.BlockSpec((tm, d), row)],
        out_shape=[jax.ShapeDtypeStruct((t, d), BF16), jax.ShapeDtypeStruct((t, d), F32)],
        compiler_params=_cparams("arbitrary"),
        name="inproj1",
    )(x2, mod3, nw, win)


def _seg_copies(seg_ref, lst_ref, goff_ref, j, local, slot, glob, sem, to_global):
    def ebody(e, carry):
        n = seg_ref[j * N_EXPERTS + e]
        s0 = lst_ref[j * N_EXPERTS + e]
        d0 = goff_ref[j * N_EXPERTS + e]

        def gbody(g, c2):
            lrow = pl.multiple_of((s0 + g) * SEG_ROWS, SEG_ROWS)
            grow = pl.multiple_of((d0 + g) * SEG_ROWS, SEG_ROWS)
            lref = local.at[slot, pl.ds(lrow, SEG_ROWS)]
            gref = glob.at[pl.ds(grow, SEG_ROWS)]
            if to_global:
                pltpu.make_async_copy(lref, gref, sem.at[slot]).start()
            else:
                pltpu.make_async_copy(gref, lref, sem.at[slot]).start()
            return c2
        lax.fori_loop(0, n, gbody, 0)
        return carry
    lax.fori_loop(0, N_EXPERTS, ebody, 0)


def _seg_wait(n, local, slot, glob, sem):
    p = local.shape[1] // SEG_ROWS
    while p >= 1:
        @pl.when((n & p) != 0)
        def _(p=p):
            rows = p * SEG_ROWS
            pltpu.make_async_copy(glob.at[pl.ds(0, rows)], local.at[slot, pl.ds(0, rows)], sem.at[slot]).wait()
        p //= 2


def _dispatch_kernel(seg_ref, lst_ref, goff_ref, ntot_ref, gap_ref, ls_ref, f_ref, xs_hbm, loc, zbuf, sem, zsem, *,
                     tm, lrows):
    j = pl.program_id(0)
    nj = pl.num_programs(0)
    slot = j % 2

    @pl.when(j >= 2)
    def _():
        _seg_wait(ntot_ref[j - 2], loc, slot, xs_hbm, sem)

    nrows = ntot_ref[j] * SEG_ROWS
    for rc in range(lrows // SORT_CHUNK):
        @pl.when(rc * SORT_CHUNK < nrows)
        def _():
            riota = lax.broadcasted_iota(jnp.int32, (SORT_CHUNK, tm), 0) + rc * SORT_CHUNK
            p = jnp.where(riota == ls_ref[0:1, :], 1.0, 0.0)
            for k in range(1, TOP_K):
                p = jnp.where(riota == ls_ref[k:k + 1, :], 1.0, p)
            loc[slot, rc * SORT_CHUNK:(rc + 1) * SORT_CHUNK, :] = _dot(p.astype(BF16), f_ref[...]).astype(BF16)

    _seg_copies(seg_ref, lst_ref, goff_ref, j, loc, slot, xs_hbm, sem, True)

    @pl.when(j == nj - 1)
    def _():
        zbuf[...] = jnp.zeros(zbuf.shape, BF16)
        zgran = zbuf.at[pl.ds(0, SEG_ROWS)]

        def zbody(e, carry):
            def gbody(g, c2):
                grow = pl.multiple_of((gap_ref[e] + g) * SEG_ROWS, SEG_ROWS)
                pltpu.make_async_copy(zgran, xs_hbm.at[pl.ds(grow, SEG_ROWS)], zsem.at[0]).start()
                return c2
            lax.fori_loop(0, gap_ref[N_EXPERTS + e], gbody, 0)
            return carry
        lax.fori_loop(0, N_EXPERTS, zbody, 0)

        def tbody(b, carry):
            brow = pl.multiple_of(b * ROUTE_BLOCK, ROUTE_BLOCK)
            pltpu.make_async_copy(zbuf, xs_hbm.at[pl.ds(brow, ROUTE_BLOCK)], zsem.at[1]).start()
            return carry
        nblocks = xs_hbm.shape[0] // ROUTE_BLOCK
        lax.fori_loop(gap_ref[2 * N_EXPERTS + 1], nblocks, tbody, 0)

        def zwait(i, carry):
            pltpu.make_async_copy(zgran, xs_hbm.at[pl.ds(0, SEG_ROWS)], zsem.at[0]).wait()
            return carry
        lax.fori_loop(0, gap_ref[2 * N_EXPERTS], zwait, 0)

        def twait(b, carry):
            pltpu.make_async_copy(zbuf, xs_hbm.at[pl.ds(0, ROUTE_BLOCK)], zsem.at[1]).wait()
            return carry
        lax.fori_loop(gap_ref[2 * N_EXPERTS + 1], nblocks, twait, 0)
        _seg_wait(ntot_ref[j], loc, slot, xs_hbm, sem)

        @pl.when(j >= 1)
        def _():
            _seg_wait(ntot_ref[j - 1], loc, 1 - slot, xs_hbm, sem)


def _dispatch(plan, ls, fb, *, tm, lrows, prows):
    t, d = fb.shape
    grid_spec = pltpu.PrefetchScalarGridSpec(
        num_scalar_prefetch=5,
        grid=(t // tm,),
        in_specs=[pl.BlockSpec((8, tm), lambda j, *_: (0, j)), pl.BlockSpec((tm, d), lambda j, *_: (j, 0))],
        out_specs=pl.BlockSpec(memory_space=pl.ANY),
        scratch_shapes=[pltpu.VMEM((2, lrows, d), BF16), pltpu.VMEM((ROUTE_BLOCK, d), BF16),
                        pltpu.SemaphoreType.DMA((2,)), pltpu.SemaphoreType.DMA((2,))],
    )
    return pl.pallas_call(
        functools.partial(_dispatch_kernel, tm=tm, lrows=lrows),
        grid_spec=grid_spec,
        out_shape=jax.ShapeDtypeStruct((prows, d), BF16),
        compiler_params=_cparams("arbitrary"),
        name="dispatch",
    )(plan["seg"], plan["lst"], plan["goff"], plan["ntot"], plan["gap"], ls, fb)


def _experts_kernel(bnd_ref, wg_ref, wu_ref, wd_ref, xs_hbm, ys_hbm, wgb, wub, wdb, xbuf, ybuf, xsem, ysem):
    e = pl.program_id(0)
    b0 = bnd_ref[e]
    b1 = bnd_ref[e + 1]
    nv = bnd_ref[N_EXPERTS]
    nblocks = xs_hbm.shape[0] // ROUTE_BLOCK

    def xcopy(b, slot):
        row = pl.multiple_of(b * ROUTE_BLOCK, ROUTE_BLOCK)
        return pltpu.make_async_copy(xs_hbm.at[pl.ds(row, ROUTE_BLOCK)], xbuf.at[slot], xsem.at[slot])

    def ycopy(b, slot):
        row = pl.multiple_of(b * ROUTE_BLOCK, ROUTE_BLOCK)
        return pltpu.make_async_copy(ybuf.at[slot], ys_hbm.at[pl.ds(row, ROUTE_BLOCK)], ysem.at[slot])

    @pl.when((e == 0) & (nv > 0))
    def _():
        xcopy(0, 0).start()

    @pl.when(b1 > b0)
    def _():
        wgb[...] = wg_ref[0].astype(BF16)
        wub[...] = wu_ref[0].astype(BF16)
        wdb[...] = wd_ref[0].astype(BF16)

    def body(b, carry):
        slot = b % 2

        @pl.when(b + 1 < nv)
        def _():
            xcopy(b + 1, 1 - slot).start()

        xcopy(b, slot).wait()

        @pl.when(b >= 2)
        def _():
            ycopy(b - 2, slot).wait()

        x = xbuf[slot]
        act = (_silu(_dot(x, wgb[...])) * _dot(x, wub[...])).astype(BF16)
        ybuf[slot] = _dot(act, wdb[...]).astype(BF16)
        ycopy(b, slot).start()
        return carry
    lax.fori_loop(b0, b1, body, 0)

    @pl.when(e == N_EXPERTS - 1)
    def _():
        @pl.when(nv >= 2)
        def _():
            ycopy(nv - 2, nv % 2).wait()

        @pl.when(nv >= 1)
        def _():
            ycopy(nv - 1, (nv - 1) % 2).wait()

        ybuf[0] = jnp.zeros(ybuf.shape[1:], BF16)

        def tbody(b, carry):
            ycopy(b, 0).start()
            return carry
        lax.fori_loop(nv, nblocks, tbody, 0)

        def twait(b, carry):
            ycopy(b, 0).wait()
            return carry
        lax.fori_loop(nv, nblocks, twait, 0)


def _experts(bounds, xs, wg, wu, wd):
    prows, d = xs.shape
    de = wg.shape[2]
    wsel = lambda e, bnd: (e, 0, 0)
    grid_spec = pltpu.PrefetchScalarGridSpec(
        num_scalar_prefetch=1,
        grid=(N_EXPERTS,),
        in_specs=[pl.BlockSpec((1, d, de), wsel), pl.BlockSpec((1, d, de), wsel), pl.BlockSpec((1, de, d), wsel),
                  pl.BlockSpec(memory_space=pl.ANY)],
        out_specs=pl.BlockSpec(memory_space=pl.ANY),
        scratch_shapes=[pltpu.VMEM((d, de), BF16), pltpu.VMEM((d, de), BF16), pltpu.VMEM((de, d), BF16),
                        pltpu.VMEM((2, ROUTE_BLOCK, d), BF16), pltpu.VMEM((2, ROUTE_BLOCK, d), BF16),
                        pltpu.SemaphoreType.DMA((2,)), pltpu.SemaphoreType.DMA((2,))],
    )
    return pl.pallas_call(
        _experts_kernel,
        grid_spec=grid_spec,
        out_shape=jax.ShapeDtypeStruct((prows, d), BF16),
        compiler_params=_cparams("arbitrary"),
        name="experts",
    )(bounds, wg, wu, wd, xs)


def _combine_kernel(seg_ref, lst_ref, goff_ref, ntot_ref, lst_t_ref, wn_t_ref, base_ref, mod_ref, fnw_ref, ys_hbm,
                    o_ref, loc, lb, wb, acc, sem, *, tm, lrows, d, final):
    j = pl.program_id(0)
    nj = pl.num_programs(0)
    slot = j % 2

    @pl.when(j == 0)
    def _():
        loc[...] = jnp.zeros(loc.shape, BF16)
        _seg_copies(seg_ref, lst_ref, goff_ref, 0, loc, 0, ys_hbm, sem, False)

    @pl.when(j + 1 < nj)
    def _():
        _seg_copies(seg_ref, lst_ref, goff_ref, j + 1, loc, 1 - slot, ys_hbm, sem, False)

    for k in range(TOP_K):
        lb[k] = jnp.broadcast_to(lst_t_ref[:, k:k + 1], (tm, HEAD_LANES))
        wb[k] = jnp.broadcast_to(wn_t_ref[:, k:k + 1], (tm, HEAD_LANES))
    acc[...] = jnp.zeros(acc.shape, F32)
    _seg_wait(ntot_ref[j], loc, slot, ys_hbm, sem)

    nrows = ntot_ref[j] * SEG_ROWS
    for rc in range(lrows // SORT_CHUNK):
        @pl.when(rc * SORT_CHUNK < nrows)
        def _():
            pieces = []
            for c in range(SORT_CHUNK // HEAD_LANES):
                ci = (lax.broadcasted_iota(jnp.int32, (tm, HEAD_LANES), 1)
                      + (rc * SORT_CHUNK + c * HEAD_LANES)).astype(F32)
                pw = jnp.where(lb[0] == ci, wb[0], 0.0)
                for k in range(1, TOP_K):
                    pw = jnp.where(lb[k] == ci, wb[k], pw)
                pieces.append(pw.astype(BF16))
            acc[...] += _dot(jnp.concatenate(pieces, axis=1), loc[slot, rc * SORT_CHUNK:(rc + 1) * SORT_CHUNK, :])

    h = base_ref[...] + mod_ref[0][:, 5 * d:6 * d] * acc[...]
    if final:
        h = _rms(h, fnw_ref[...])
    o_ref[...] = h


def _combine(plan, lst_t, wn_t, ys, base, mod3, fnw, *, tm, lrows, seq, final):
    t, d = base.shape
    full = lambda j, *_: (0, 0)
    row = lambda j, *_: (j, 0)
    grid_spec = pltpu.PrefetchScalarGridSpec(
        num_scalar_prefetch=4,
        grid=(t // tm,),
        in_specs=[pl.BlockSpec((tm, 8), row), pl.BlockSpec((tm, 8), row), pl.BlockSpec((tm, d), row),
                  pl.BlockSpec((1, 1, 6 * d), lambda j, *_: ((j * tm) // seq, 0, 0)),
                  pl.BlockSpec((1, d), full), pl.BlockSpec(memory_space=pl.ANY)],
        out_specs=pl.BlockSpec((tm, d), row),
        scratch_shapes=[pltpu.VMEM((2, lrows, d), BF16), pltpu.VMEM((TOP_K, tm, HEAD_LANES), F32),
                        pltpu.VMEM((TOP_K, tm, HEAD_LANES), F32), pltpu.VMEM((tm, d), F32),
                        pltpu.SemaphoreType.DMA((2,))],
    )
    return pl.pallas_call(
        functools.partial(_combine_kernel, tm=tm, lrows=lrows, d=d, final=final),
        grid_spec=grid_spec,
        out_shape=jax.ShapeDtypeStruct((t, d), F32),
        compiler_params=_cparams("arbitrary"),
        name="combine_final" if final else "combine",
    )(plan["seg"], plan["lst"], plan["goff"], plan["ntot"], lst_t, wn_t, base, mod3, fnw, ys)


def _dispatch_plan(seg_out):
    seg = seg_out[:, :, 0].astype(jnp.int32)
    gpb = ROUTE_BLOCK // SEG_ROWS
    lst = jnp.cumsum(seg, axis=1) - seg
    tile_off = jnp.cumsum(seg, axis=0) - seg
    tot = jnp.sum(seg, axis=0)
    region_blk = (tot + gpb - 1) // gpb
    gend_blk = jnp.cumsum(region_blk)
    gstart = (gend_blk - region_blk) * gpb
    goff = gstart[None, :] + tile_off
    gap_cnt = region_blk * gpb - tot
    gap = jnp.concatenate([gstart + tot, gap_cnt, jnp.sum(gap_cnt)[None], gend_blk[-1:]]).astype(jnp.int32)
    bounds = jnp.concatenate([jnp.zeros((1,), jnp.int32), gend_blk.astype(jnp.int32)])
    plan = {"seg": seg.reshape(-1), "lst": lst.reshape(-1).astype(jnp.int32),
            "goff": goff.reshape(-1).astype(jnp.int32), "ntot": jnp.sum(seg, axis=1).astype(jnp.int32), "gap": gap}
    return plan, bounds


def _rope_tables(n):
    rows = n // GRID_W
    row = jnp.repeat(jnp.arange(rows), GRID_W).astype(F32)
    col = jnp.tile(jnp.arange(GRID_W), rows).astype(F32)
    half = QK_ROPE // 2
    inv = 1.0 / (ROPE_BASE ** (jnp.arange(0, half, 2, dtype=F32) / half))
    ang = jnp.stack([row[:, None] * inv, col[:, None] * inv], axis=1)
    cos, sin = jnp.cos(ang), jnp.sin(ang)
    c32 = jnp.broadcast_to(cos[:, :, None, :], (n, 2, 2, half // 2)).reshape(n, QK_ROPE)
    s32 = jnp.broadcast_to(sin[:, :, None, :], (n, 2, 2, half // 2)).reshape(n, QK_ROPE)
    return c32, s32


def _swap_signed(w):
    wr = w.reshape(w.shape[:-1] + (2, 2, QK_ROPE // 4))
    return jnp.stack([-wr[..., 1, :], wr[..., 0, :]], axis=-2).reshape(w.shape)


def _head_group(parts, lead):
    width = sum(p.shape[-1] for p in parts)
    pad = jnp.zeros(lead + (N_HEADS, HEAD_LANES - width), F32)
    return jnp.concatenate(list(parts) + [pad], axis=-1).reshape(lead + (N_HEADS * HEAD_LANES,))


def kernel(x, c, ctx, c_ctx, ada_w, ada_b, norm_mix_w, norm_ffn_w, ev_w_in, ev_conv_w, ev_conv_b, ev_ln_w,
           ev_ln_b, ev_q_norm_w, ev_kv_norm_w, ev_w_uq, ev_w_ukv, ev_w_out, od_w_in, od_conv_w, od_w_out,
           router_w, router_bias, exp_w_gate, exp_w_up, exp_w_down, sh_w_gate, sh_w_up, sh_w_down,
           final_norm_w):
    bsz, n, d = x.shape
    nctx = ctx.shape[1]
    t = bsz * n
    tm = 512
    x2 = x.reshape(t, d)

    c16 = jnp.zeros((16, d), F32).at[0:bsz].set(c).at[bsz].set(c_ctx)
    mod = _ada(c16, ada_w, ada_b)
    mod0 = mod[0].reshape(16, 1, 6 * d)
    mod1 = mod[1].reshape(16, 1, 6 * d)

    w_in = ev_w_in[0]
    o_kr = 2 * CONV_CH + Q_LORA + KV_LORA
    w_kr = w_in[:, o_kr:o_kr + QK_ROPE]
    win_ext = jnp.concatenate(
        [w_in[:, :o_kr], w_kr, _swap_signed(w_kr), jnp.zeros((d, HEAD_LANES - 2 * QK_ROPE), F32)], axis=1).astype(BF16)
    wuq = ev_w_uq[0]
    wq_rope = wuq[..., QK_NOPE:]
    zq = jnp.zeros((Q_LORA, N_HEADS, QK_NOPE), F32)
    wq_ext = jnp.concatenate([_head_group([wuq], (Q_LORA,)),
                              _head_group([zq, _swap_signed(wq_rope)], (Q_LORA,))], axis=1).astype(BF16)
    wukv = ev_w_ukv[0]
    wkv_ext = jnp.concatenate([_head_group([wukv[..., :QK_NOPE]], (KV_LORA,)),
                               _head_group([wukv[..., QK_NOPE:]], (KV_LORA,))], axis=1).astype(BF16)
    eye = jnp.eye(QK_ROPE, dtype=F32)
    e_small = _head_group([jnp.zeros((QK_ROPE, N_HEADS, QK_NOPE), F32),
                           jnp.broadcast_to(eye[:, None, :], (QK_ROPE, N_HEADS, QK_ROPE))], (QK_ROPE,))
    emat = jnp.concatenate([e_small, jnp.zeros((HEAD_LANES - QK_ROPE, N_HEADS * HEAD_LANES), F32)], axis=0).astype(BF16)

    c32, s32 = _rope_tables(n)
    scale = (QK_NOPE + QK_ROPE) ** -0.5 * float(np.log2(np.e))
    zpad = jnp.zeros((n, HEAD_LANES - 2 * QK_ROPE), F32)
    ck_lat = jnp.concatenate([c32, s32, zpad], axis=1)
    zrope = jnp.zeros((n, HEAD_LANES - QK_NOPE - QK_ROPE), F32)
    cq_lat = jnp.concatenate([jnp.full((n, QK_NOPE), scale, F32), c32 * scale, zrope,
                              jnp.zeros((n, QK_NOPE), F32), s32 * scale, zrope], axis=1)
    ck_ctx = jnp.concatenate([jnp.ones((nctx, QK_ROPE), F32), jnp.zeros((nctx, HEAD_LANES - QK_ROPE), F32)], axis=1)
    cq_ctx = jnp.zeros((nctx, 2 * HEAD_LANES), F32)

    nw0 = norm_mix_w[0].reshape(1, d)
    qnw = ev_q_norm_w[0].reshape(1, Q_LORA)
    kvnw = ev_kv_norm_w[0].reshape(1, KV_LORA)
    glu, q, k_l, v_l = _inproj0(x2, mod0, lambda i: (i * tm) // n, nw0, win_ext, qnw, kvnw, wq_ext, wkv_ext,
                                emat, ck_lat, cq_lat, latent=True, tm=tm, seq=n)
    tmc = min(tm, nctx)
    k_c, v_c = _inproj0(ctx.reshape(bsz * nctx, d), mod0, lambda i: bsz, nw0,
                        win_ext[:, 2 * CONV_CH + Q_LORA:], qnw, kvnw, wq_ext, wkv_ext,
                        emat, ck_ctx, cq_ctx, latent=False, tm=tmc, seq=nctx)

    o_a = _conv(glu, ev_conv_w[0].reshape(CONV_WIDTH, CONV_CH), ev_conv_b[0].reshape(1, CONV_CH),
                ev_ln_w[0].reshape(1, CONV_CH), ev_ln_b[0].reshape(1, CONV_CH), tn=tm, seq=n)
    nk = N_HEADS * HEAD_LANES
    o_b = _attn(q.reshape(bsz, n, nk), k_c.reshape(bsz, nctx, nk), k_l.reshape(bsz, n, nk),
                v_c.reshape(bsz, nctx, nk), v_l.reshape(bsz, n, nk), tq=256)
    o_b = o_b.reshape(t, N_HEADS * V_HEAD)

    utri = (jnp.arange(tm)[:, None] < jnp.arange(tm)[None, :]).astype(BF16)
    ltri = (jnp.arange(N_EXPERTS)[None, :] < jnp.arange(N_EXPERTS)[:, None]).astype(BF16)

    def tail_weights(i):
        rw = jnp.concatenate([router_w[i], jnp.zeros((d, HEAD_LANES - N_EXPERTS), F32)], axis=1)
        rwh = rw.astype(BF16)
        rwl = (rw - rwh.astype(F32)).astype(BF16)
        return (norm_ffn_w[i].reshape(1, d), sh_w_gate[i].astype(BF16), sh_w_up[i].astype(BF16),
                sh_w_down[i].astype(BF16), rwh, rwl, router_bias[i].astype(F32).reshape(N_EXPERTS, 1), utri, ltri)

    ntiles = t // tm
    max_pad = SEG_ROWS - 1
    lrows = -(-(TOP_K * tm + N_EXPERTS * max_pad) // SORT_CHUNK) * SORT_CHUNK
    nblocks = -(-(TOP_K * t + ntiles * N_EXPERTS * max_pad + N_EXPERTS * (ROUTE_BLOCK - SEG_ROWS)) // ROUTE_BLOCK)
    fnw = final_norm_w.reshape(1, d)

    def moe(i, fb, base, ls, wn, seg_out, modi, final):
        plan, bounds = _dispatch_plan(seg_out)
        xs = _dispatch(plan, ls, fb, tm=tm, lrows=lrows, prows=nblocks * ROUTE_BLOCK)
        ys = _experts(bounds, xs, exp_w_gate[i], exp_w_up[i], exp_w_down[i])
        return _combine(plan, ls.T.astype(F32), wn.T, ys, base, modi, fnw, tm=tm, lrows=lrows, seq=n, final=final)

    w_out = ev_w_out[0].astype(BF16)
    tail0 = _mix0(o_a, o_b, x2, mod0, w_out[:CONV_CH], w_out[CONV_CH:], tail_weights(0), tm=tm, seq=n)
    h = moe(0, *tail0, mod0, False)

    bq, z = _inproj1(h, mod1, norm_mix_w[1].reshape(1, d), od_w_in[0].astype(BF16), tm=tm, seq=n)
    tail1 = _mix1(bq, z, h, mod1, od_conv_w[0].reshape(3, d), od_w_out[0].astype(BF16), tail_weights(1), tm=tm, seq=n)
    out = moe(1, *tail1, mod1, True)
    return out.reshape(bsz, n, d)
```

```python
import functools

import jax
import jax.numpy as jnp
import numpy as np
from jax import lax
from jax.experimental import pallas as pl
from jax.experimental.pallas import tpu as pltpu

F32 = jnp.float32
BF16 = jnp.bfloat16

EPS = 1e-6
GRID_W = 64
CONV_CH = 512
CONV_WIDTH = 31
N_HEADS = 8
QK_NOPE = 64
QK_ROPE = 32
V_HEAD = 64
Q_LORA = 256
KV_LORA = 128
ROPE_BASE = 10000.0
N_EXPERTS = 64
TOP_K = 6
N_GROUPS = 8
TOPK_GROUPS = 4
ROUTED_SCALE = 2.5
ROUTE_BLOCK = 256

HEAD_LANES = 128
SEG_ROWS = 16
SORT_CHUNK = 256
VMEM_LIMIT = 56 * 1024 * 1024


def _cparams(*sem):
    return pltpu.CompilerParams(dimension_semantics=sem, vmem_limit_bytes=VMEM_LIMIT)


def _dot(a, b):
    return jnp.dot(a, b, preferred_element_type=F32)


def _dot_nt(a, b):
    return lax.dot_general(a, b, (((1,), (1,)), ((), ())), preferred_element_type=F32)


def _rms(x, w):
    return x * lax.rsqrt(jnp.mean(x * x, axis=-1, keepdims=True) + EPS) * w


def _silu(x):
    return x * jax.nn.sigmoid(x)


def _ada_kernel(c_ref, w_ref, b_ref, o_ref):
    c = c_ref[...]
    o_ref[0] = _dot(_silu(c), w_ref[0]) + b_ref[0]


def _ada(c16, ada_w, ada_b):
    depth, d, n6 = ada_w.shape
    tn = 1536
    return pl.pallas_call(
        _ada_kernel,
        grid=(depth, n6 // tn),
        in_specs=[
            pl.BlockSpec((16, d), lambda l, j: (0, 0)),
            pl.BlockSpec((1, d, tn), lambda l, j: (l, 0, j)),
            pl.BlockSpec((1, 1, tn), lambda l, j: (l, 0, j)),
        ],
        out_specs=pl.BlockSpec((1, 16, tn), lambda l, j: (l, 0, j)),
        out_shape=jax.ShapeDtypeStruct((depth, 16, n6), F32),
        compiler_params=_cparams("arbitrary", "arbitrary"),
        name="ada",
    )(c16, ada_w, ada_b.reshape(depth, 1, n6))


def _inproj0_kernel(x_ref, mod_ref, nw_ref, win_ref, qnw_ref, kvnw_ref, wq_ref, wkv_ref, e_ref,
                    ck_ref, cq_ref, *outs, latent, d):
    x = x_ref[...]
    mod = mod_ref[0]
    a = (_rms(x, nw_ref[...]) * (1.0 + mod[:, d:2 * d]) + mod[:, 0:d]).astype(BF16)
    u = _dot(a, win_ref[...])
    if latent:
        glu_ref, q_ref, k_ref, v_ref = outs
        glu_ref[...] = u[:, 0:CONV_CH] * jax.nn.sigmoid(u[:, CONV_CH:2 * CONV_CH])
        o = 2 * CONV_CH
        cqn = _rms(u[:, o:o + Q_LORA], qnw_ref[...]).astype(BF16)
        o += Q_LORA
        cosq = cq_ref[:, 0:HEAD_LANES]
        sinq = cq_ref[:, HEAD_LANES:2 * HEAD_LANES]
        nq = N_HEADS * HEAD_LANES
        qa = _dot(cqn, wq_ref[:, 0:nq])
        qb = _dot(cqn, wq_ref[:, nq:2 * nq])
        for h in range(N_HEADS):
            sl = slice(h * HEAD_LANES, (h + 1) * HEAD_LANES)
            q_ref[:, sl] = (qa[:, sl] * cosq + qb[:, sl] * sinq).astype(BF16)
    else:
        k_ref, v_ref = outs
        o = 0
    nk = N_HEADS * HEAD_LANES
    kvn = _rms(u[:, o:o + KV_LORA], kvnw_ref[...]).astype(BF16)
    o += KV_LORA
    prod = u[:, o:o + HEAD_LANES] * ck_ref[...]
    krot = (prod + pltpu.roll(prod, HEAD_LANES - QK_ROPE, 1)).astype(BF16)
    k = _dot(kvn, wkv_ref[:, 0:nk]) + _dot(krot, e_ref[...])
    k_ref[...] = k.astype(BF16)
    lane = lax.broadcasted_iota(jnp.int32, (1, nk), 1)
    ones = jnp.where(lane % HEAD_LANES == V_HEAD, 1.0, 0.0).astype(F32)
    v_ref[...] = (_dot(kvn, wkv_ref[:, nk:2 * nk]) + ones).astype(BF16)


def _inproj0(x2, mod3, mod_row_fn, nw, win, qnw, kvnw, wq, wkv, emat, ck, cq, *, latent, tm, seq):
    t, d = x2.shape
    nk = N_HEADS * HEAD_LANES
    tiles_per_seq = seq // tm
    full = lambda i: (0, 0)
    in_specs = [
        pl.BlockSpec((tm, d), lambda i: (i, 0)),
        pl.BlockSpec((1, 1, 6 * d), lambda i: (mod_row_fn(i), 0, 0)),
        pl.BlockSpec((1, d), full),
        pl.BlockSpec(win.shape, full),
        pl.BlockSpec((1, Q_LORA), full),
        pl.BlockSpec((1, KV_LORA), full),
        pl.BlockSpec(wq.shape, full),
        pl.BlockSpec(wkv.shape, full),
        pl.BlockSpec(emat.shape, full),
        pl.BlockSpec((tm, HEAD_LANES), lambda i: (i % tiles_per_seq, 0)),
        pl.BlockSpec((tm, 2 * HEAD_LANES), lambda i: (i % tiles_per_seq, 0)),
    ]
    row = lambda i: (i, 0)
    out_specs = [pl.BlockSpec((tm, nk), row), pl.BlockSpec((tm, nk), row)]
    out_shape = [jax.ShapeDtypeStruct((t, nk), BF16), jax.ShapeDtypeStruct((t, nk), BF16)]
    if latent:
        out_specs = [pl.BlockSpec((tm, CONV_CH), row), pl.BlockSpec((tm, nk), row)] + out_specs
        out_shape = [jax.ShapeDtypeStruct((t, CONV_CH), F32), jax.ShapeDtypeStruct((t, nk), BF16)] + out_shape
    return pl.pallas_call(
        functools.partial(_inproj0_kernel, latent=latent, d=d),
        grid=(t // tm,),
        in_specs=in_specs,
        out_specs=out_specs,
        out_shape=out_shape,
        compiler_params=_cparams("arbitrary"),
        name="inproj0_lat" if latent else "inproj0_ctx",
    )(x2, mod3, nw, win, qnw, kvnw, wq, wkv, emat, ck, cq)


CONV_HALO = 16
CONV_ROWS = 64


def _conv_kernel(main_ref, prev_ref, next_ref, w_ref, b_ref, lnw_ref, lnb_ref, o_ref, win_ref, sh_ref, *,
                 tn, tiles_per_seq):
    i = pl.program_id(0)
    first = (i % tiles_per_seq) == 0
    last = (i % tiles_per_seq) == tiles_per_seq - 1
    win_ref[0:CONV_HALO, :] = jnp.where(first, 0.0, prev_ref[...])
    win_ref[CONV_HALO:CONV_HALO + tn, :] = main_ref[...]
    win_ref[CONV_HALO + tn:, :] = jnp.where(last, 0.0, next_ref[...])
    shl = sh_ref.shape[1]
    for r in range(1, 8):
        sh_ref[r - 1] = win_ref[r:r + shl, :]
    off0 = CONV_HALO - CONV_WIDTH // 2
    for c in range(tn // CONV_ROWS):
        acc = None
        for k in range(CONV_WIDTH):
            a, r = divmod(off0 + k, 8)
            lo = c * CONV_ROWS + 8 * a
            src = win_ref[lo:lo + CONV_ROWS, :] if r == 0 else sh_ref[r - 1, lo:lo + CONV_ROWS, :]
            term = src * w_ref[k:k + 1, :]
            acc = term if acc is None else acc + term
        y = acc + b_ref[...]
        mu = jnp.mean(y, axis=-1, keepdims=True)
        yc = y - mu
        var = jnp.mean(yc * yc, axis=-1, keepdims=True)
        yn = yc * lax.rsqrt(var + EPS) * lnw_ref[...] + lnb_ref[...]
        o_ref[c * CONV_ROWS:(c + 1) * CONV_ROWS, :] = _silu(yn).astype(BF16)


def _conv(glu, w, b, lnw, lnb, *, tn, seq):
    t, ch = glu.shape
    tiles_per_seq = seq // tn
    hb = tn // CONV_HALO
    nhb = t // CONV_HALO
    full = lambda i: (0, 0)
    return pl.pallas_call(
        functools.partial(_conv_kernel, tn=tn, tiles_per_seq=tiles_per_seq),
        grid=(t // tn,),
        in_specs=[
            pl.BlockSpec((tn, ch), lambda i: (i, 0)),
            pl.BlockSpec((CONV_HALO, ch), lambda i: (jnp.maximum(i * hb - 1, 0), 0)),
            pl.BlockSpec((CONV_HALO, ch), lambda i: (jnp.minimum((i + 1) * hb, nhb - 1), 0)),
            pl.BlockSpec(w.shape, full),
            pl.BlockSpec((1, ch), full),
            pl.BlockSpec((1, ch), full),
            pl.BlockSpec((1, ch), full),
        ],
        out_specs=pl.BlockSpec((tn, ch), lambda i: (i, 0)),
        out_shape=jax.ShapeDtypeStruct((t, ch), BF16),
        scratch_shapes=[pltpu.VMEM((tn + 2 * CONV_HALO, ch), F32),
                        pltpu.VMEM((7, tn + 2 * CONV_HALO - 8, ch), F32)],
        compiler_params=_cparams("arbitrary"),
        name="conv",
    )(glu, glu, glu, w, b, lnw, lnb)


ATT_KC = 512


ATT_HEADS = 4


def _attn_kernel(q_ref, kc_ref, kl_ref, vc_ref, vl_ref, o_ref, s_ref, *, nctx, nlat):
    nchunk = nlat // ATT_KC

    def keys(c):
        return (slice(0, nctx), slice(0, nctx)) if c < 0 else \
            (slice(c * ATT_KC, (c + 1) * ATT_KC), slice(nctx + c * ATT_KC, nctx + (c + 1) * ATT_KC))

    def score_chunk(h, c, m):
        sl = slice(h * HEAD_LANES, (h + 1) * HEAD_LANES)
        rows, cols = keys(c)
        k = kc_ref[0, rows, sl] if c < 0 else kl_ref[0, rows, sl]
        s = _dot_nt(q_ref[0, :, sl], k)
        s_ref[h % 2, :, cols] = s
        mc = jnp.max(s, axis=-1, keepdims=True)
        return mc if m is None else jnp.maximum(m, mc)

    def value_chunk(h, c, m, acc):
        sl = slice(h * HEAD_LANES, (h + 1) * HEAD_LANES)
        rows, cols = keys(c)
        v = vc_ref[0, rows, sl] if c < 0 else vl_ref[0, rows, sl]
        p = jnp.exp2((s_ref[h % 2, :, cols] - m).astype(BF16))
        pv = _dot(p, v)
        return pv if acc is None else acc + pv

    m = None
    for c in range(-1, nchunk):
        m = score_chunk(0, c, m)
    outs = []
    for h in range(ATT_HEADS):
        acc, m_next = None, None
        for c in range(-1, nchunk):
            acc = value_chunk(h, c, m, acc)
            if h + 1 < ATT_HEADS:
                m_next = score_chunk(h + 1, c, m_next)
        m = m_next
        outs.append(acc[:, 0:V_HEAD] / acc[:, V_HEAD:V_HEAD + 1])
    o_ref[0] = jnp.concatenate(outs, axis=-1).astype(BF16)


def _attn(q, kc, kl, vc, vl, *, tq):
    b, n, _ = q.shape
    nctx = kc.shape[1]
    hp = N_HEADS // ATT_HEADS
    w = ATT_HEADS * HEAD_LANES
    return pl.pallas_call(
        functools.partial(_attn_kernel, nctx=nctx, nlat=n),
        grid=(b, hp, n // tq),
        in_specs=[
            pl.BlockSpec((1, tq, w), lambda bi, j, i: (bi, i, j)),
            pl.BlockSpec((1, nctx, w), lambda bi, j, i: (bi, 0, j)),
            pl.BlockSpec((1, n, w), lambda bi, j, i: (bi, 0, j)),
            pl.BlockSpec((1, nctx, w), lambda bi, j, i: (bi, 0, j)),
            pl.BlockSpec((1, n, w), lambda bi, j, i: (bi, 0, j)),
        ],
        out_specs=pl.BlockSpec((1, tq, ATT_HEADS * V_HEAD), lambda bi, j, i: (bi, i, j)),
        out_shape=jax.ShapeDtypeStruct((b, n, N_HEADS * V_HEAD), BF16),
        scratch_shapes=[pltpu.VMEM((2, tq, nctx + n), F32)],
        compiler_params=_cparams("arbitrary", "arbitrary", "arbitrary"),
        name="attn",
    )(q, kc, kl, vc, vl)


def _route(lt, rb_ref, utri_ref, ltri_ref, ls_ref, wn_ref, seg_ref, tm):
    ninf = -jnp.inf
    gsz = N_EXPERTS // N_GROUPS
    s = jax.nn.sigmoid(lt)
    sel = s + rb_ref[...]
    sub = lax.broadcasted_iota(jnp.int32, (gsz, tm), 0).astype(F32)
    gs = []
    for g in range(N_GROUPS):
        xg = sel[g * gsz:(g + 1) * gsz]
        m1 = jnp.max(xg, axis=0, keepdims=True)
        i1 = jnp.min(jnp.where(xg == m1, sub, float(gsz)), axis=0, keepdims=True)
        m2 = jnp.max(jnp.where(sub == i1, ninf, xg), axis=0, keepdims=True)
        gs.append(m1 + m2)
    rows = []
    for g in range(N_GROUPS):
        beaten = jnp.zeros((1, tm), F32)
        for o in range(N_GROUPS):
            if o != g:
                beats = (gs[o] >= gs[g]) if o < g else (gs[o] > gs[g])
                beaten = beaten + jnp.where(beats, 1.0, 0.0)
        rows.append(jnp.where(beaten < float(TOPK_GROUPS), sel[g * gsz:(g + 1) * gsz], ninf))
    cur = jnp.concatenate(rows, axis=0)
    ei = lax.broadcasted_iota(jnp.int32, (N_EXPERTS, tm), 0).astype(F32)
    hits, ws = [], []
    for k in range(TOP_K):
        m = jnp.max(cur, axis=0, keepdims=True)
        idx = jnp.min(jnp.where(cur == m, ei, float(N_EXPERTS)), axis=0, keepdims=True)
        hit = ei == idx
        ws.append(jnp.sum(jnp.where(hit, s, 0.0), axis=0, keepdims=True))
        cur = jnp.where(hit, ninf, cur)
        hits.append(jnp.where(hit, 1.0, 0.0))
    wsum = ws[0]
    for k in range(1, TOP_K):
        wsum = wsum + ws[k]
    pre = _dot(jnp.concatenate(hits, axis=0).astype(BF16), utri_ref[...])
    tots = [jnp.sum(h, axis=1, keepdims=True) for h in hits]
    cnt = tots[0]
    for k in range(1, TOP_K):
        cnt = cnt + tots[k]
    seg = jnp.floor((cnt + (SEG_ROWS - 1.0)) * (1.0 / SEG_ROWS))
    segb = jnp.broadcast_to(seg, (N_EXPERTS, HEAD_LANES))
    seg_ref[0] = segb
    lstart = _dot(ltri_ref[...], segb.astype(BF16))[:, 0:1] * float(SEG_ROWS)
    basek = lstart
    for k in range(TOP_K):
        rank = jnp.sum(hits[k] * (pre[k * N_EXPERTS:(k + 1) * N_EXPERTS] + basek), axis=0, keepdims=True)
        ls_ref[k:k + 1, :] = rank.astype(jnp.int32)
        wn_ref[k:k + 1, :] = ws[k] / wsum * ROUTED_SCALE
        basek = basek + tots[k]
    ls_ref[TOP_K:8, :] = jnp.full((8 - TOP_K, tm), -1, jnp.int32)
    wn_ref[TOP_K:8, :] = jnp.zeros((8 - TOP_K, tm), F32)


def _tail(y, x, mod, d, tm, nfw_ref, swg_ref, swu_ref, swd_ref, rwh_ref, rwl_ref, rb_ref, utri_ref, ltri_ref,
          fb_ref, base_ref, ls_ref, wn_ref, seg_ref):
    g1 = mod[:, 2 * d:3 * d]
    sh2 = mod[:, 3 * d:4 * d]
    sc2 = mod[:, 4 * d:5 * d]
    g2 = mod[:, 5 * d:6 * d]
    h1 = x + g1 * y
    f = _rms(h1, nfw_ref[...]) * (1.0 + sc2) + sh2
    fb = f.astype(BF16)
    fb_ref[...] = fb
    act = (_silu(_dot(fb, swg_ref[...])) * _dot(fb, swu_ref[...])).astype(BF16)
    base_ref[...] = h1 + g2 * _dot(act, swd_ref[...])
    flo = (f - fb.astype(F32)).astype(BF16)
    logits = _dot(fb, rwh_ref[...]) + (_dot(flo, rwh_ref[...]) + _dot(fb, rwl_ref[...]))
    _route(logits.T[0:N_EXPERTS], rb_ref, utri_ref, ltri_ref, ls_ref, wn_ref, seg_ref, tm)


def _mix0_kernel(oa_ref, ob_ref, x_ref, mod_ref, woa_ref, wob_ref, *rest, d, tm):
    y = _dot(oa_ref[...], woa_ref[...]) + _dot(ob_ref[...], wob_ref[...])
    _tail(y, x_ref[...], mod_ref[0], d, tm, *rest)


def _mix1_kernel(b_ref, z_ref, zp_ref, zn_ref, x_ref, mod_ref, cw_ref, wo_ref, *rest, d, tm, tiles_per_seq):
    i = pl.program_id(0)
    first = (i % tiles_per_seq) == 0
    last = (i % tiles_per_seq) == tiles_per_seq - 1
    z = z_ref[...].astype(F32)
    rowi = lax.broadcasted_iota(jnp.int32, (tm, 1), 0)
    zprev_row = jnp.where(first, 0.0, zp_ref[7:8, :].astype(F32))
    znext_row = jnp.where(last, 0.0, zn_ref[0:1, :].astype(F32))
    zm = jnp.where(rowi == 0, zprev_row, pltpu.roll(z, 1, 0))
    zp = jnp.where(rowi == tm - 1, znext_row, pltpu.roll(z, tm - 1, 0))
    conv = zm * cw_ref[0:1, :] + z * cw_ref[1:2, :] + zp * cw_ref[2:3, :]
    v = (b_ref[...].astype(F32) * conv).astype(BF16)
    _tail(_dot(v, wo_ref[...]), x_ref[...], mod_ref[0], d, tm, *rest)


def _tail_specs(t, d, tm, tail_w):
    full = lambda i: (0, 0)
    row = lambda i: (i, 0)
    col = lambda i: (0, i)
    in_specs = [pl.BlockSpec(w.shape, full) for w in tail_w]
    out_specs = [pl.BlockSpec((tm, d), row), pl.BlockSpec((tm, d), row), pl.BlockSpec((8, tm), col),
                 pl.BlockSpec((8, tm), col), pl.BlockSpec((1, N_EXPERTS, HEAD_LANES), lambda i: (i, 0, 0))]
    out_shape = [jax.ShapeDtypeStruct((t, d), BF16), jax.ShapeDtypeStruct((t, d), F32),
                 jax.ShapeDtypeStruct((8, t), jnp.int32), jax.ShapeDtypeStruct((8, t), F32),
                 jax.ShapeDtypeStruct((t // tm, N_EXPERTS, HEAD_LANES), F32)]
    return in_specs, out_specs, out_shape


def _mix0(oa, ob, x2, mod3, woa, wob, tail_w, *, tm, seq):
    t, d = x2.shape
    full = lambda i: (0, 0)
    row = lambda i: (i, 0)
    tin, out_specs, out_shape = _tail_specs(t, d, tm, tail_w)
    return pl.pallas_call(
        functools.partial(_mix0_kernel, d=d, tm=tm),
        grid=(t // tm,),
        in_specs=[pl.BlockSpec((tm, oa.shape[1]), row), pl.BlockSpec((tm, ob.shape[1]), row),
                  pl.BlockSpec((tm, d), row),
                  pl.BlockSpec((1, 1, 6 * d), lambda i: ((i * tm) // seq, 0, 0)),
                  pl.BlockSpec(woa.shape, full), pl.BlockSpec(wob.shape, full)] + tin,
        out_specs=out_specs,
        out_shape=out_shape,
        compiler_params=_cparams("arbitrary"),
        name="mix0_tail",
    )(oa, ob, x2, mod3, woa, wob, *tail_w)


def _mix1(bq, z, x2, mod3, cw, wo, tail_w, *, tm, seq):
    t, d = x2.shape
    full = lambda i: (0, 0)
    row = lambda i: (i, 0)
    hb = tm // 8
    nhb = t // 8
    tin, out_specs, out_shape = _tail_specs(t, d, tm, tail_w)
    return pl.pallas_call(
        functools.partial(_mix1_kernel, d=d, tm=tm, tiles_per_seq=seq // tm),
        grid=(t // tm,),
        in_specs=[pl.BlockSpec((tm, d), row), pl.BlockSpec((tm, d), row),
                  pl.BlockSpec((8, d), lambda i: (jnp.maximum(i * hb - 1, 0), 0)),
                  pl.BlockSpec((8, d), lambda i: (jnp.minimum((i + 1) * hb, nhb - 1), 0)),
                  pl.BlockSpec((tm, d), row),
                  pl.BlockSpec((1, 1, 6 * d), lambda i: ((i * tm) // seq, 0, 0)),
                  pl.BlockSpec(cw.shape, full), pl.BlockSpec(wo.shape, full)] + tin,
        out_specs=out_specs,
        out_shape=out_shape,
        compiler_params=_cparams("arbitrary"),
        name="mix1_tail",
    )(bq, z, z, z, x2, mod3, cw, wo, *tail_w)


def _inproj1_kernel(x_ref, mod_ref, nw_ref, win_ref, b_ref, z_ref, *, d):
    mod = mod_ref[0]
    a = (_rms(x_ref[...], nw_ref[...]) * (1.0 + mod[:, d:2 * d]) + mod[:, 0:d]).astype(BF16)
    u = _dot(a, win_ref[...])
    b_ref[...] = u[:, 0:d].astype(BF16)
    z_ref[...] = (u[:, d:2 * d] * u[:, 2 * d:3 * d]).astype(F32)


def _inproj1(x2, mod3, nw, win, *, tm, seq):
    t, d = x2.shape
    full = lambda i: (0, 0)
    row = lambda i: (i, 0)
    return pl.pallas_call(
        functools.partial(_inproj1_kernel, d=d),
        grid=(t // tm,),
        in_specs=[pl.BlockSpec((tm, d), row),
                  pl.BlockSpec((1, 1, 6 * d), lambda i: ((i * tm) // seq, 0, 0)),
                  pl.BlockSpec((1, d), full), pl.BlockSpec(win.shape, full)],
        out_specs=[pl.BlockSpec((tm, d), row), pl.BlockSpec((tm, d), row)],
        out_shape=[jax.ShapeDtypeStruct((t, d), BF16), jax.ShapeDtypeStruct((t, d), F32)],
        compiler_params=_cparams("arbitrary"),
        name="inproj1",
    )(x2, mod3, nw, win)


def _seg_copies(seg_ref, lst_ref, goff_ref, j, local, slot, glob, sem, to_global):
    def ebody(e, carry):
        n = seg_ref[j * N_EXPERTS + e]
        s0 = lst_ref[j * N_EXPERTS + e]
        d0 = goff_ref[j * N_EXPERTS + e]

        def gbody(g, c2):
            lrow = pl.multiple_of((s0 + g) * SEG_ROWS, SEG_ROWS)
            grow = pl.multiple_of((d0 + g) * SEG_ROWS, SEG_ROWS)
            lref = local.at[slot, pl.ds(lrow, SEG_ROWS)]
            gref = glob.at[pl.ds(grow, SEG_ROWS)]
            if to_global:
                pltpu.make_async_copy(lref, gref, sem.at[slot]).start()
            else:
                pltpu.make_async_copy(gref, lref, sem.at[slot]).start()
            return c2
        lax.fori_loop(0, n, gbody, 0)
        return carry
    lax.fori_loop(0, N_EXPERTS, ebody, 0)


def _seg_wait(n, local, slot, glob, sem):
    p = local.shape[1] // SEG_ROWS
    while p >= 1:
        @pl.when((n & p) != 0)
        def _(p=p):
            rows = p * SEG_ROWS
            pltpu.make_async_copy(glob.at[pl.ds(0, rows)], local.at[slot, pl.ds(0, rows)], sem.at[slot]).wait()
        p //= 2


def _sorted_onehot(ls_f, rc, vals):
    tm = ls_f.shape[1]
    riota = lax.broadcasted_iota(jnp.int32, (SORT_CHUNK, tm), 0).astype(F32).astype(BF16)
    rel = jnp.clip(ls_f - float(rc * SORT_CHUNK), -1.0, float(SORT_CHUNK)).astype(BF16)
    p = jnp.zeros((SORT_CHUNK, tm), BF16)
    for k in range(TOP_K):
        val = jnp.ones((1, tm), BF16) if vals is None else vals[k:k + 1, :]
        p = jnp.where(riota == rel[k:k + 1, :], val, p)
    return p


def _dispatch_kernel(seg_ref, lst_ref, goff_ref, ntot_ref, gap_ref, ls_ref, f_ref, xs_hbm, loc, zbuf, sem, zsem, *,
                     tm, lrows):
    j = pl.program_id(0)
    nj = pl.num_programs(0)
    slot = j % 2

    @pl.when(j >= 2)
    def _():
        _seg_wait(ntot_ref[j - 2], loc, slot, xs_hbm, sem)

    nrows = ntot_ref[j] * SEG_ROWS
    ls_f = ls_ref[...].astype(F32)
    for rc in range(lrows // SORT_CHUNK):
        @pl.when(rc * SORT_CHUNK < nrows)
        def _():
            p = _sorted_onehot(ls_f, rc, None)
            loc[slot, rc * SORT_CHUNK:(rc + 1) * SORT_CHUNK, :] = _dot(p, f_ref[...]).astype(BF16)

    _seg_copies(seg_ref, lst_ref, goff_ref, j, loc, slot, xs_hbm, sem, True)

    @pl.when(j == nj - 1)
    def _():
        zbuf[...] = jnp.zeros(zbuf.shape, BF16)
        zgran = zbuf.at[pl.ds(0, SEG_ROWS)]

        def zbody(e, carry):
            def gbody(g, c2):
                grow = pl.multiple_of((gap_ref[e] + g) * SEG_ROWS, SEG_ROWS)
                pltpu.make_async_copy(zgran, xs_hbm.at[pl.ds(grow, SEG_ROWS)], zsem.at[0]).start()
                return c2
            lax.fori_loop(0, gap_ref[N_EXPERTS + e], gbody, 0)
            return carry
        lax.fori_loop(0, N_EXPERTS, zbody, 0)

        def tbody(b, carry):
            brow = pl.multiple_of(b * ROUTE_BLOCK, ROUTE_BLOCK)
            pltpu.make_async_copy(zbuf, xs_hbm.at[pl.ds(brow, ROUTE_BLOCK)], zsem.at[1]).start()
            return carry
        nblocks = xs_hbm.shape[0] // ROUTE_BLOCK
        lax.fori_loop(gap_ref[2 * N_EXPERTS + 1], nblocks, tbody, 0)

        def zwait(i, carry):
            pltpu.make_async_copy(zgran, xs_hbm.at[pl.ds(0, SEG_ROWS)], zsem.at[0]).wait()
            return carry
        lax.fori_loop(0, gap_ref[2 * N_EXPERTS], zwait, 0)

        def twait(b, carry):
            pltpu.make_async_copy(zbuf, xs_hbm.at[pl.ds(0, ROUTE_BLOCK)], zsem.at[1]).wait()
            return carry
        lax.fori_loop(gap_ref[2 * N_EXPERTS + 1], nblocks, twait, 0)
        _seg_wait(ntot_ref[j], loc, slot, xs_hbm, sem)

        @pl.when(j >= 1)
        def _():
            _seg_wait(ntot_ref[j - 1], loc, 1 - slot, xs_hbm, sem)


def _dispatch(plan, ls, fb, *, tm, lrows, prows):
    t, d = fb.shape
    grid_spec = pltpu.PrefetchScalarGridSpec(
        num_scalar_prefetch=5,
        grid=(t // tm,),
        in_specs=[pl.BlockSpec((8, tm), lambda j, *_: (0, j)), pl.BlockSpec((tm, d), lambda j, *_: (j, 0))],
        out_specs=pl.BlockSpec(memory_space=pl.ANY),
        scratch_shapes=[pltpu.VMEM((2, lrows, d), BF16), pltpu.VMEM((ROUTE_BLOCK, d), BF16),
                        pltpu.SemaphoreType.DMA((2,)), pltpu.SemaphoreType.DMA((2,))],
    )
    return pl.pallas_call(
        functools.partial(_dispatch_kernel, tm=tm, lrows=lrows),
        grid_spec=grid_spec,
        out_shape=jax.ShapeDtypeStruct((prows, d), BF16),
        compiler_params=_cparams("arbitrary"),
        name="dispatch",
    )(plan["seg"], plan["lst"], plan["goff"], plan["ntot"], plan["gap"], ls, fb)


def _experts_kernel(bnd_ref, wg_ref, wu_ref, wd_ref, xs_hbm, ys_hbm, wgb, wub, wdb, xbuf, ybuf, xsem, ysem):
    e = pl.program_id(0)
    b0 = bnd_ref[e]
    b1 = bnd_ref[e + 1]
    nv = bnd_ref[N_EXPERTS]
    nblocks = xs_hbm.shape[0] // ROUTE_BLOCK

    def xcopy(b, slot):
        row = pl.multiple_of(b * ROUTE_BLOCK, ROUTE_BLOCK)
        return pltpu.make_async_copy(xs_hbm.at[pl.ds(row, ROUTE_BLOCK)], xbuf.at[slot], xsem.at[slot])

    def ycopy(b, slot):
        row = pl.multiple_of(b * ROUTE_BLOCK, ROUTE_BLOCK)
        return pltpu.make_async_copy(ybuf.at[slot], ys_hbm.at[pl.ds(row, ROUTE_BLOCK)], ysem.at[slot])

    @pl.when((e == 0) & (nv > 0))
    def _():
        xcopy(0, 0).start()

    @pl.when(b1 > b0)
    def _():
        wgb[...] = wg_ref[0, 0].astype(BF16)
        wub[...] = wu_ref[0, 0].astype(BF16)
        wdb[...] = wd_ref[0, 0].astype(BF16)

    def body(b, carry):
        slot = b % 2

        @pl.when(b + 1 < nv)
        def _():
            xcopy(b + 1, 1 - slot).start()

        xcopy(b, slot).wait()

        @pl.when(b >= 2)
        def _():
            ycopy(b - 2, slot).wait()

        x = xbuf[slot]
        act = (_silu(_dot(x, wgb[...])) * _dot(x, wub[...])).astype(BF16)
        ybuf[slot] = _dot(act, wdb[...]).astype(BF16)
        ycopy(b, slot).start()
        return carry
    lax.fori_loop(b0, b1, body, 0)

    @pl.when(e == N_EXPERTS - 1)
    def _():
        @pl.when(nv >= 2)
        def _():
            ycopy(nv - 2, nv % 2).wait()

        @pl.when(nv >= 1)
        def _():
            ycopy(nv - 1, (nv - 1) % 2).wait()

        ybuf[0] = jnp.zeros(ybuf.shape[1:], BF16)

        def tbody(b, carry):
            ycopy(b, 0).start()
            return carry
        lax.fori_loop(nv, nblocks, tbody, 0)

        def twait(b, carry):
            ycopy(b, 0).wait()
            return carry
        lax.fori_loop(nv, nblocks, twait, 0)


def _experts(bounds, xs, wg, wu, wd, layer):
    prows, d = xs.shape
    de = wg.shape[3]
    wsel = lambda e, bnd: (layer, e, 0, 0)
    grid_spec = pltpu.PrefetchScalarGridSpec(
        num_scalar_prefetch=1,
        grid=(N_EXPERTS,),
        in_specs=[pl.BlockSpec((1, 1, d, de), wsel), pl.BlockSpec((1, 1, d, de), wsel),
                  pl.BlockSpec((1, 1, de, d), wsel), pl.BlockSpec(memory_space=pl.ANY)],
        out_specs=pl.BlockSpec(memory_space=pl.ANY),
        scratch_shapes=[pltpu.VMEM((d, de), BF16), pltpu.VMEM((d, de), BF16), pltpu.VMEM((de, d), BF16),
                        pltpu.VMEM((2, ROUTE_BLOCK, d), BF16), pltpu.VMEM((2, ROUTE_BLOCK, d), BF16),
                        pltpu.SemaphoreType.DMA((2,)), pltpu.SemaphoreType.DMA((2,))],
    )
    return pl.pallas_call(
        _experts_kernel,
        grid_spec=grid_spec,
        out_shape=jax.ShapeDtypeStruct((prows, d), BF16),
        compiler_params=_cparams("arbitrary"),
        name="experts",
    )(bounds, wg, wu, wd, xs)


def _combine_kernel(seg_ref, lst_ref, goff_ref, ntot_ref, ls_ref, wn_ref, base_ref, mod_ref, fnw_ref, ys_hbm,
                    o_ref, loc, pw, acc, sem, *, tm, lrows, d, final):
    j = pl.program_id(0)
    nj = pl.num_programs(0)
    slot = j % 2

    @pl.when(j == 0)
    def _():
        loc[...] = jnp.zeros(loc.shape, BF16)
        _seg_copies(seg_ref, lst_ref, goff_ref, 0, loc, 0, ys_hbm, sem, False)

    @pl.when(j + 1 < nj)
    def _():
        _seg_copies(seg_ref, lst_ref, goff_ref, j + 1, loc, 1 - slot, ys_hbm, sem, False)

    ls_f = ls_ref[...].astype(F32)
    wn_b = wn_ref[...].astype(BF16)
    nfix = TOP_K * tm // SORT_CHUNK
    for rc in range(nfix):
        pw[rc * SORT_CHUNK:(rc + 1) * SORT_CHUNK, :] = _sorted_onehot(ls_f, rc, wn_b)
    _seg_wait(ntot_ref[j], loc, slot, ys_hbm, sem)
    tn_dims = (((0,), (0,)), ((), ()))
    acc[...] = lax.dot_general(pw[...], loc[slot, 0:nfix * SORT_CHUNK, :], tn_dims, preferred_element_type=F32)

    nrows = ntot_ref[j] * SEG_ROWS
    for rc in range(nfix, lrows // SORT_CHUNK):
        @pl.when(rc * SORT_CHUNK < nrows)
        def _():
            ysc = loc[slot, rc * SORT_CHUNK:(rc + 1) * SORT_CHUNK, :]
            acc[...] += lax.dot_general(_sorted_onehot(ls_f, rc, wn_b), ysc, tn_dims, preferred_element_type=F32)

    h = base_ref[...] + mod_ref[0][:, 5 * d:6 * d] * acc[...]
    if final:
        h = _rms(h, fnw_ref[...])
    o_ref[...] = h


def _combine(plan, ls, wn, ys, base, mod3, fnw, *, tm, lrows, seq, final):
    t, d = base.shape
    full = lambda j, *_: (0, 0)
    row = lambda j, *_: (j, 0)
    col = lambda j, *_: (0, j)
    grid_spec = pltpu.PrefetchScalarGridSpec(
        num_scalar_prefetch=4,
        grid=(t // tm,),
        in_specs=[pl.BlockSpec((8, tm), col), pl.BlockSpec((8, tm), col), pl.BlockSpec((tm, d), row),
                  pl.BlockSpec((1, 1, 6 * d), lambda j, *_: ((j * tm) // seq, 0, 0)),
                  pl.BlockSpec((1, d), full), pl.BlockSpec(memory_space=pl.ANY)],
        out_specs=pl.BlockSpec((tm, d), row),
        scratch_shapes=[pltpu.VMEM((2, lrows, d), BF16), pltpu.VMEM((TOP_K * tm, tm), BF16),
                        pltpu.VMEM((tm, d), F32), pltpu.SemaphoreType.DMA((2,))],
    )
    return pl.pallas_call(
        functools.partial(_combine_kernel, tm=tm, lrows=lrows, d=d, final=final),
        grid_spec=grid_spec,
        out_shape=jax.ShapeDtypeStruct((t, d), F32),
        compiler_params=_cparams("arbitrary"),
        name="combine_final" if final else "combine",
    )(plan["seg"], plan["lst"], plan["goff"], plan["ntot"], ls, wn, base, mod3, fnw, ys)


def _dispatch_plan(seg_out):
    seg = seg_out[:, :, 0].astype(jnp.int32)
    gpb = ROUTE_BLOCK // SEG_ROWS
    lst = jnp.cumsum(seg, axis=1) - seg
    tile_off = jnp.cumsum(seg, axis=0) - seg
    tot = jnp.sum(seg, axis=0)
    region_blk = (tot + gpb - 1) // gpb
    gend_blk = jnp.cumsum(region_blk)
    gstart = (gend_blk - region_blk) * gpb
    goff = gstart[None, :] + tile_off
    gap_cnt = region_blk * gpb - tot
    gap = jnp.concatenate([gstart + tot, gap_cnt, jnp.sum(gap_cnt)[None], gend_blk[-1:]]).astype(jnp.int32)
    bounds = jnp.concatenate([jnp.zeros((1,), jnp.int32), gend_blk.astype(jnp.int32)])
    plan = {"seg": seg.reshape(-1), "lst": lst.reshape(-1).astype(jnp.int32),
            "goff": goff.reshape(-1).astype(jnp.int32), "ntot": jnp.sum(seg, axis=1).astype(jnp.int32), "gap": gap}
    return plan, bounds


def _rope_tables(n):
    rows = n // GRID_W
    row = jnp.repeat(jnp.arange(rows), GRID_W).astype(F32)
    col = jnp.tile(jnp.arange(GRID_W), rows).astype(F32)
    half = QK_ROPE // 2
    inv = 1.0 / (ROPE_BASE ** (jnp.arange(0, half, 2, dtype=F32) / half))
    ang = jnp.stack([row[:, None] * inv, col[:, None] * inv], axis=1)
    cos, sin = jnp.cos(ang), jnp.sin(ang)
    c32 = jnp.broadcast_to(cos[:, :, None, :], (n, 2, 2, half // 2)).reshape(n, QK_ROPE)
    s32 = jnp.broadcast_to(sin[:, :, None, :], (n, 2, 2, half // 2)).reshape(n, QK_ROPE)
    return c32, s32


def _swap_signed(w):
    wr = w.reshape(w.shape[:-1] + (2, 2, QK_ROPE // 4))
    return jnp.stack([-wr[..., 1, :], wr[..., 0, :]], axis=-2).reshape(w.shape)


def _head_group(parts, lead):
    width = sum(p.shape[-1] for p in parts)
    pad = jnp.zeros(lead + (N_HEADS, HEAD_LANES - width), F32)
    return jnp.concatenate(list(parts) + [pad], axis=-1).reshape(lead + (N_HEADS * HEAD_LANES,))


def kernel(x, c, ctx, c_ctx, ada_w, ada_b, norm_mix_w, norm_ffn_w, ev_w_in, ev_conv_w, ev_conv_b, ev_ln_w,
           ev_ln_b, ev_q_norm_w, ev_kv_norm_w, ev_w_uq, ev_w_ukv, ev_w_out, od_w_in, od_conv_w, od_w_out,
           router_w, router_bias, exp_w_gate, exp_w_up, exp_w_down, sh_w_gate, sh_w_up, sh_w_down,
           final_norm_w):
    bsz, n, d = x.shape
    nctx = ctx.shape[1]
    t = bsz * n
    tm = 512
    x2 = x.reshape(t, d)

    c16 = jnp.zeros((16, d), F32).at[0:bsz].set(c).at[bsz].set(c_ctx)
    mod = _ada(c16, ada_w, ada_b)
    mod0 = mod[0].reshape(16, 1, 6 * d)
    mod1 = mod[1].reshape(16, 1, 6 * d)

    w_in = ev_w_in[0]
    o_kr = 2 * CONV_CH + Q_LORA + KV_LORA
    w_kr = w_in[:, o_kr:o_kr + QK_ROPE]
    win_ext = jnp.concatenate(
        [w_in[:, :o_kr], w_kr, _swap_signed(w_kr), jnp.zeros((d, HEAD_LANES - 2 * QK_ROPE), F32)], axis=1).astype(BF16)
    wuq = ev_w_uq[0]
    wq_rope = wuq[..., QK_NOPE:]
    zq = jnp.zeros((Q_LORA, N_HEADS, QK_NOPE), F32)
    wq_ext = jnp.concatenate([_head_group([wuq], (Q_LORA,)),
                              _head_group([zq, _swap_signed(wq_rope)], (Q_LORA,))], axis=1).astype(BF16)
    wukv = ev_w_ukv[0]
    wkv_ext = jnp.concatenate([_head_group([wukv[..., :QK_NOPE]], (KV_LORA,)),
                               _head_group([wukv[..., QK_NOPE:]], (KV_LORA,))], axis=1).astype(BF16)
    eye = jnp.eye(QK_ROPE, dtype=F32)
    e_small = _head_group([jnp.zeros((QK_ROPE, N_HEADS, QK_NOPE), F32),
                           jnp.broadcast_to(eye[:, None, :], (QK_ROPE, N_HEADS, QK_ROPE))], (QK_ROPE,))
    emat = jnp.concatenate([e_small, jnp.zeros((HEAD_LANES - QK_ROPE, N_HEADS * HEAD_LANES), F32)], axis=0).astype(BF16)

    c32, s32 = _rope_tables(n)
    scale = (QK_NOPE + QK_ROPE) ** -0.5 * float(np.log2(np.e))
    zpad = jnp.zeros((n, HEAD_LANES - 2 * QK_ROPE), F32)
    ck_lat = jnp.concatenate([c32, s32, zpad], axis=1)
    zrope = jnp.zeros((n, HEAD_LANES - QK_NOPE - QK_ROPE), F32)
    cq_lat = jnp.concatenate([jnp.full((n, QK_NOPE), scale, F32), c32 * scale, zrope,
                              jnp.zeros((n, QK_NOPE), F32), s32 * scale, zrope], axis=1)
    ck_ctx = jnp.concatenate([jnp.ones((nctx, QK_ROPE), F32), jnp.zeros((nctx, HEAD_LANES - QK_ROPE), F32)], axis=1)
    cq_ctx = jnp.zeros((nctx, 2 * HEAD_LANES), F32)

    nw0 = norm_mix_w[0].reshape(1, d)
    qnw = ev_q_norm_w[0].reshape(1, Q_LORA)
    kvnw = ev_kv_norm_w[0].reshape(1, KV_LORA)
    glu, q, k_l, v_l = _inproj0(x2, mod0, lambda i: (i * tm) // n, nw0, win_ext, qnw, kvnw, wq_ext, wkv_ext,
                                emat, ck_lat, cq_lat, latent=True, tm=tm, seq=n)
    tmc = min(tm, nctx)
    k_c, v_c = _inproj0(ctx.reshape(bsz * nctx, d), mod0, lambda i: bsz, nw0,
                        win_ext[:, 2 * CONV_CH + Q_LORA:], qnw, kvnw, wq_ext, wkv_ext,
                        emat, ck_ctx, cq_ctx, latent=False, tm=tmc, seq=nctx)

    o_a = _conv(glu, ev_conv_w[0].reshape(CONV_WIDTH, CONV_CH), ev_conv_b[0].reshape(1, CONV_CH),
                ev_ln_w[0].reshape(1, CONV_CH), ev_ln_b[0].reshape(1, CONV_CH), tn=tm, seq=n)
    nk = N_HEADS * HEAD_LANES
    o_b = _attn(q.reshape(bsz, n, nk), k_c.reshape(bsz, nctx, nk), k_l.reshape(bsz, n, nk),
                v_c.reshape(bsz, nctx, nk), v_l.reshape(bsz, n, nk), tq=256)
    o_b = o_b.reshape(t, N_HEADS * V_HEAD)

    utri = (jnp.arange(tm)[:, None] < jnp.arange(tm)[None, :]).astype(BF16)
    ltri = (jnp.arange(N_EXPERTS)[None, :] < jnp.arange(N_EXPERTS)[:, None]).astype(BF16)

    def tail_weights(i):
        rw = jnp.concatenate([router_w[i], jnp.zeros((d, HEAD_LANES - N_EXPERTS), F32)], axis=1)
        rwh = rw.astype(BF16)
        rwl = (rw - rwh.astype(F32)).astype(BF16)
        return (norm_ffn_w[i].reshape(1, d), sh_w_gate[i].astype(BF16), sh_w_up[i].astype(BF16),
                sh_w_down[i].astype(BF16), rwh, rwl, router_bias[i].astype(F32).reshape(N_EXPERTS, 1), utri, ltri)

    ntiles = t // tm
    max_pad = SEG_ROWS - 1
    lrows = -(-(TOP_K * tm + N_EXPERTS * max_pad) // SORT_CHUNK) * SORT_CHUNK
    nblocks = -(-(TOP_K * t + ntiles * N_EXPERTS * max_pad + N_EXPERTS * (ROUTE_BLOCK - SEG_ROWS)) // ROUTE_BLOCK)
    fnw = final_norm_w.reshape(1, d)

    def moe(i, fb, base, ls, wn, seg_out, modi, final):
        plan, bounds = _dispatch_plan(seg_out)
        xs = _dispatch(plan, ls, fb, tm=tm, lrows=lrows, prows=nblocks * ROUTE_BLOCK)
        ys = _experts(bounds, xs, exp_w_gate, exp_w_up, exp_w_down, i)
        return _combine(plan, ls, wn, ys, base, modi, fnw, tm=tm, lrows=lrows, seq=n, final=final)

    w_out = ev_w_out[0].astype(BF16)
    tail0 = _mix0(o_a, o_b, x2, mod0, w_out[:CONV_CH], w_out[CONV_CH:], tail_weights(0), tm=tm, seq=n)
    h = moe(0, *tail0, mod0, False)

    bq, z = _inproj1(h, mod1, norm_mix_w[1].reshape(1, d), od_w_in[0].astype(BF16), tm=tm, seq=n)
    tail1 = _mix1(bq, z, h, mod1, od_conv_w[0].reshape(3, d), od_w_out[0].astype(BF16), tail_weights(1), tm=tm, seq=n)
    out = moe(1, *tail1, mod1, True)
    return out.reshape(bsz, n, d)
```

```python
import functools

import jax
import jax.numpy as jnp
import numpy as np
from jax import lax
from jax.experimental import pallas as pl
from jax.experimental.pallas import tpu as pltpu

F32 = jnp.float32
BF16 = jnp.bfloat16

EPS = 1e-6
GRID_W = 64
CONV_CH = 512
CONV_WIDTH = 31
N_HEADS = 8
QK_NOPE = 64
QK_ROPE = 32
V_HEAD = 64
Q_LORA = 256
KV_LORA = 128
ROPE_BASE = 10000.0
N_EXPERTS = 64
TOP_K = 6
N_GROUPS = 8
TOPK_GROUPS = 4
ROUTED_SCALE = 2.5
ROUTE_BLOCK = 256

HEAD_LANES = 128
SEG_ROWS = 16
SORT_CHUNK = 256
VMEM_LIMIT = 56 * 1024 * 1024


def _cparams(*sem, **kw):
    return pltpu.CompilerParams(dimension_semantics=sem, vmem_limit_bytes=VMEM_LIMIT, **kw)


def _dot(a, b):
    return jnp.dot(a, b, preferred_element_type=F32)


def _dot_nt(a, b):
    return lax.dot_general(a, b, (((1,), (1,)), ((), ())), preferred_element_type=F32)


def _rms(x, w):
    return x * lax.rsqrt(jnp.mean(x * x, axis=-1, keepdims=True) + EPS) * w


def _silu(x):
    return x * jax.nn.sigmoid(x)


def _ada_kernel(c_ref, w_ref, b_ref, o_ref):
    c = c_ref[...]
    o_ref[0] = _dot(_silu(c), w_ref[0]) + b_ref[0]


def _ada(c16, ada_w, ada_b):
    depth, d, n6 = ada_w.shape
    tn = 1536
    return pl.pallas_call(
        _ada_kernel,
        grid=(depth, n6 // tn),
        in_specs=[
            pl.BlockSpec((16, d), lambda l, j: (0, 0)),
            pl.BlockSpec((1, d, tn), lambda l, j: (l, 0, j)),
            pl.BlockSpec((1, 1, tn), lambda l, j: (l, 0, j)),
        ],
        out_specs=pl.BlockSpec((1, 16, tn), lambda l, j: (l, 0, j)),
        out_shape=jax.ShapeDtypeStruct((depth, 16, n6), F32),
        compiler_params=_cparams("arbitrary", "arbitrary"),
        name="ada",
    )(c16, ada_w, ada_b.reshape(depth, 1, n6))


def _inproj0_kernel(x_ref, mod_ref, nw_ref, win_ref, qnw_ref, kvnw_ref, wq_ref, wkv_ref, e_ref,
                    ck_ref, cq_ref, *outs, latent, d):
    x = x_ref[...]
    mod = mod_ref[0]
    a = (_rms(x, nw_ref[...]) * (1.0 + mod[:, d:2 * d]) + mod[:, 0:d]).astype(BF16)
    u = _dot(a, win_ref[...])
    if latent:
        glu_ref, q_ref, k_ref, v_ref = outs
        glu_ref[...] = u[:, 0:CONV_CH] * jax.nn.sigmoid(u[:, CONV_CH:2 * CONV_CH])
        o = 2 * CONV_CH
        cqn = _rms(u[:, o:o + Q_LORA], qnw_ref[...]).astype(BF16)
        o += Q_LORA
        cosq = cq_ref[:, 0:HEAD_LANES]
        sinq = cq_ref[:, HEAD_LANES:2 * HEAD_LANES]
        nq = N_HEADS * HEAD_LANES
        qa = _dot(cqn, wq_ref[:, 0:nq])
        qb = _dot(cqn, wq_ref[:, nq:2 * nq])
        for h in range(N_HEADS):
            sl = slice(h * HEAD_LANES, (h + 1) * HEAD_LANES)
            q_ref[:, sl] = (qa[:, sl] * cosq + qb[:, sl] * sinq).astype(BF16)
    else:
        k_ref, v_ref = outs
        o = 0
    nk = N_HEADS * HEAD_LANES
    kvn = _rms(u[:, o:o + KV_LORA], kvnw_ref[...]).astype(BF16)
    o += KV_LORA
    prod = u[:, o:o + HEAD_LANES] * ck_ref[...]
    krot = (prod + pltpu.roll(prod, HEAD_LANES - QK_ROPE, 1)).astype(BF16)
    k = _dot(kvn, wkv_ref[:, 0:nk]) + _dot(krot, e_ref[...])
    k_ref[...] = k.astype(BF16)
    lane = lax.broadcasted_iota(jnp.int32, (1, nk), 1)
    ones = jnp.where(lane % HEAD_LANES == V_HEAD, 1.0, 0.0).astype(F32)
    v_ref[...] = (_dot(kvn, wkv_ref[:, nk:2 * nk]) + ones).astype(BF16)


def _inproj0(x2, mod3, mod_row_fn, nw, win, qnw, kvnw, wq, wkv, emat, ck, cq, *, latent, tm, seq):
    t, d = x2.shape
    nk = N_HEADS * HEAD_LANES
    tiles_per_seq = seq // tm
    full = lambda i: (0, 0)
    in_specs = [
        pl.BlockSpec((tm, d), lambda i: (i, 0)),
        pl.BlockSpec((1, 1, 6 * d), lambda i: (mod_row_fn(i), 0, 0)),
        pl.BlockSpec((1, d), full),
        pl.BlockSpec(win.shape, full),
        pl.BlockSpec((1, Q_LORA), full),
        pl.BlockSpec((1, KV_LORA), full),
        pl.BlockSpec(wq.shape, full),
        pl.BlockSpec(wkv.shape, full),
        pl.BlockSpec(emat.shape, full),
        pl.BlockSpec((tm, HEAD_LANES), lambda i: (i % tiles_per_seq, 0)),
        pl.BlockSpec((tm, 2 * HEAD_LANES), lambda i: (i % tiles_per_seq, 0)),
    ]
    row = lambda i: (i, 0)
    out_specs = [pl.BlockSpec((tm, nk), row), pl.BlockSpec((tm, nk), row)]
    out_shape = [jax.ShapeDtypeStruct((t, nk), BF16), jax.ShapeDtypeStruct((t, nk), BF16)]
    if latent:
        out_specs = [pl.BlockSpec((tm, CONV_CH), row), pl.BlockSpec((tm, nk), row)] + out_specs
        out_shape = [jax.ShapeDtypeStruct((t, CONV_CH), F32), jax.ShapeDtypeStruct((t, nk), BF16)] + out_shape
    return pl.pallas_call(
        functools.partial(_inproj0_kernel, latent=latent, d=d),
        grid=(t // tm,),
        in_specs=in_specs,
        out_specs=out_specs,
        out_shape=out_shape,
        compiler_params=_cparams("arbitrary"),
        name="inproj0_lat" if latent else "inproj0_ctx",
    )(x2, mod3, nw, win, qnw, kvnw, wq, wkv, emat, ck, cq)


CONV_HALO = 16
CONV_ROWS = 64


def _conv_kernel(main_ref, prev_ref, next_ref, w_ref, b_ref, lnw_ref, lnb_ref, o_ref, win_ref, sh_ref, *,
                 tn, tiles_per_seq):
    i = pl.program_id(0)
    first = (i % tiles_per_seq) == 0
    last = (i % tiles_per_seq) == tiles_per_seq - 1
    win_ref[0:CONV_HALO, :] = jnp.where(first, 0.0, prev_ref[...])
    win_ref[CONV_HALO:CONV_HALO + tn, :] = main_ref[...]
    win_ref[CONV_HALO + tn:, :] = jnp.where(last, 0.0, next_ref[...])
    shl = sh_ref.shape[1]
    for r in range(1, 8):
        sh_ref[r - 1] = win_ref[r:r + shl, :]
    off0 = CONV_HALO - CONV_WIDTH // 2
    for c in range(tn // CONV_ROWS):
        acc = None
        for k in range(CONV_WIDTH):
            a, r = divmod(off0 + k, 8)
            lo = c * CONV_ROWS + 8 * a
            src = win_ref[lo:lo + CONV_ROWS, :] if r == 0 else sh_ref[r - 1, lo:lo + CONV_ROWS, :]
            term = src * w_ref[k:k + 1, :]
            acc = term if acc is None else acc + term
        y = acc + b_ref[...]
        mu = jnp.mean(y, axis=-1, keepdims=True)
        yc = y - mu
        var = jnp.mean(yc * yc, axis=-1, keepdims=True)
        yn = yc * lax.rsqrt(var + EPS) * lnw_ref[...] + lnb_ref[...]
        o_ref[c * CONV_ROWS:(c + 1) * CONV_ROWS, :] = _silu(yn).astype(BF16)


def _conv(glu, w, b, lnw, lnb, *, tn, seq):
    t, ch = glu.shape
    tiles_per_seq = seq // tn
    hb = tn // CONV_HALO
    nhb = t // CONV_HALO
    full = lambda i: (0, 0)
    return pl.pallas_call(
        functools.partial(_conv_kernel, tn=tn, tiles_per_seq=tiles_per_seq),
        grid=(t // tn,),
        in_specs=[
            pl.BlockSpec((tn, ch), lambda i: (i, 0)),
            pl.BlockSpec((CONV_HALO, ch), lambda i: (jnp.maximum(i * hb - 1, 0), 0)),
            pl.BlockSpec((CONV_HALO, ch), lambda i: (jnp.minimum((i + 1) * hb, nhb - 1), 0)),
            pl.BlockSpec(w.shape, full),
            pl.BlockSpec((1, ch), full),
            pl.BlockSpec((1, ch), full),
            pl.BlockSpec((1, ch), full),
        ],
        out_specs=pl.BlockSpec((tn, ch), lambda i: (i, 0)),
        out_shape=jax.ShapeDtypeStruct((t, ch), BF16),
        scratch_shapes=[pltpu.VMEM((tn + 2 * CONV_HALO, ch), F32),
                        pltpu.VMEM((7, tn + 2 * CONV_HALO - 8, ch), F32)],
        compiler_params=_cparams("arbitrary"),
        name="conv",
    )(glu, glu, glu, w, b, lnw, lnb)


ATT_KC = 512


ATT_HEADS = 4


def _attn_kernel(q_ref, kc_ref, kl_ref, vc_ref, vl_ref, o_ref, s_ref, *, nctx, nlat):
    nchunk = nlat // ATT_KC

    def keys(c):
        return (slice(0, nctx), slice(0, nctx)) if c < 0 else \
            (slice(c * ATT_KC, (c + 1) * ATT_KC), slice(nctx + c * ATT_KC, nctx + (c + 1) * ATT_KC))

    def score_chunk(h, c, m):
        sl = slice(h * HEAD_LANES, (h + 1) * HEAD_LANES)
        rows, cols = keys(c)
        k = kc_ref[0, rows, sl] if c < 0 else kl_ref[0, rows, sl]
        s = _dot_nt(q_ref[0, :, sl], k)
        s_ref[h % 2, :, cols] = s
        mc = jnp.max(s, axis=-1, keepdims=True)
        return mc if m is None else jnp.maximum(m, mc)

    def value_chunk(h, c, m, acc):
        sl = slice(h * HEAD_LANES, (h + 1) * HEAD_LANES)
        rows, cols = keys(c)
        v = vc_ref[0, rows, sl] if c < 0 else vl_ref[0, rows, sl]
        p = jnp.exp2((s_ref[h % 2, :, cols] - m).astype(BF16))
        pv = _dot(p, v)
        return pv if acc is None else acc + pv

    m = None
    for c in range(-1, nchunk):
        m = score_chunk(0, c, m)
    outs = []
    for h in range(ATT_HEADS):
        acc, m_next = None, None
        for c in range(-1, nchunk):
            acc = value_chunk(h, c, m, acc)
            if h + 1 < ATT_HEADS:
                m_next = score_chunk(h + 1, c, m_next)
        m = m_next
        outs.append(acc[:, 0:V_HEAD] / acc[:, V_HEAD:V_HEAD + 1])
    o_ref[0] = jnp.concatenate(outs, axis=-1).astype(BF16)


def _attn(q, kc, kl, vc, vl, *, tq):
    b, n, _ = q.shape
    nctx = kc.shape[1]
    hp = N_HEADS // ATT_HEADS
    w = ATT_HEADS * HEAD_LANES
    return pl.pallas_call(
        functools.partial(_attn_kernel, nctx=nctx, nlat=n),
        grid=(b, hp, n // tq),
        in_specs=[
            pl.BlockSpec((1, tq, w), lambda bi, j, i: (bi, i, j)),
            pl.BlockSpec((1, nctx, w), lambda bi, j, i: (bi, 0, j)),
            pl.BlockSpec((1, n, w), lambda bi, j, i: (bi, 0, j)),
            pl.BlockSpec((1, nctx, w), lambda bi, j, i: (bi, 0, j)),
            pl.BlockSpec((1, n, w), lambda bi, j, i: (bi, 0, j)),
        ],
        out_specs=pl.BlockSpec((1, tq, ATT_HEADS * V_HEAD), lambda bi, j, i: (bi, i, j)),
        out_shape=jax.ShapeDtypeStruct((b, n, N_HEADS * V_HEAD), BF16),
        scratch_shapes=[pltpu.VMEM((2, tq, nctx + n), F32)],
        compiler_params=_cparams("arbitrary", "arbitrary", "arbitrary"),
        name="attn",
    )(q, kc, kl, vc, vl)


def _route(lt, rb_ref, utri_ref, ltri_ref, ls_ref, wn_ref, seg_ref, tm):
    ninf = -jnp.inf
    gsz = N_EXPERTS // N_GROUPS
    s = jax.nn.sigmoid(lt)
    sel = s + rb_ref[...]
    sub = lax.broadcasted_iota(jnp.int32, (gsz, tm), 0).astype(F32)
    gs = []
    for g in range(N_GROUPS):
        xg = sel[g * gsz:(g + 1) * gsz]
        m1 = jnp.max(xg, axis=0, keepdims=True)
        i1 = jnp.min(jnp.where(xg == m1, sub, float(gsz)), axis=0, keepdims=True)
        m2 = jnp.max(jnp.where(sub == i1, ninf, xg), axis=0, keepdims=True)
        gs.append(m1 + m2)
    rows = []
    for g in range(N_GROUPS):
        beaten = jnp.zeros((1, tm), F32)
        for o in range(N_GROUPS):
            if o != g:
                beats = (gs[o] >= gs[g]) if o < g else (gs[o] > gs[g])
                beaten = beaten + jnp.where(beats, 1.0, 0.0)
        rows.append(jnp.where(beaten < float(TOPK_GROUPS), sel[g * gsz:(g + 1) * gsz], ninf))
    cur = jnp.concatenate(rows, axis=0)
    ei = lax.broadcasted_iota(jnp.int32, (N_EXPERTS, tm), 0).astype(F32)
    hits, ws = [], []
    for k in range(TOP_K):
        m = jnp.max(cur, axis=0, keepdims=True)
        idx = jnp.min(jnp.where(cur == m, ei, float(N_EXPERTS)), axis=0, keepdims=True)
        hit = ei == idx
        ws.append(jnp.sum(jnp.where(hit, s, 0.0), axis=0, keepdims=True))
        cur = jnp.where(hit, ninf, cur)
        hits.append(jnp.where(hit, 1.0, 0.0))
    wsum = ws[0]
    for k in range(1, TOP_K):
        wsum = wsum + ws[k]
    pre = _dot(jnp.concatenate(hits, axis=0).astype(BF16), utri_ref[...])
    tots = [jnp.sum(h, axis=1, keepdims=True) for h in hits]
    cnt = tots[0]
    for k in range(1, TOP_K):
        cnt = cnt + tots[k]
    seg = jnp.floor((cnt + (SEG_ROWS - 1.0)) * (1.0 / SEG_ROWS))
    segb = jnp.broadcast_to(seg, (N_EXPERTS, HEAD_LANES))
    seg_ref[0] = segb
    lstart = _dot(ltri_ref[...], segb.astype(BF16))[:, 0:1] * float(SEG_ROWS)
    basek = lstart
    for k in range(TOP_K):
        rank = jnp.sum(hits[k] * (pre[k * N_EXPERTS:(k + 1) * N_EXPERTS] + basek), axis=0, keepdims=True)
        ls_ref[k:k + 1, :] = rank.astype(jnp.int32)
        wn_ref[k:k + 1, :] = ws[k] / wsum * ROUTED_SCALE
        basek = basek + tots[k]
    ls_ref[TOP_K:8, :] = jnp.full((8 - TOP_K, tm), -1, jnp.int32)
    wn_ref[TOP_K:8, :] = jnp.zeros((8 - TOP_K, tm), F32)


def _tail(y, x, mod, d, tm, nfw_ref, swg_ref, swu_ref, swd_ref, rwh_ref, rwl_ref, rb_ref, utri_ref, ltri_ref,
          fb_ref, base_ref, ls_ref, wn_ref, seg_ref):
    g1 = mod[:, 2 * d:3 * d]
    sh2 = mod[:, 3 * d:4 * d]
    sc2 = mod[:, 4 * d:5 * d]
    g2 = mod[:, 5 * d:6 * d]
    h1 = x + g1 * y
    f = _rms(h1, nfw_ref[...]) * (1.0 + sc2) + sh2
    fb = f.astype(BF16)
    fb_ref[...] = fb
    act = (_silu(_dot(fb, swg_ref[...])) * _dot(fb, swu_ref[...])).astype(BF16)
    base_ref[...] = h1 + g2 * _dot(act, swd_ref[...])
    flo = (f - fb.astype(F32)).astype(BF16)
    logits = _dot(fb, rwh_ref[...]) + (_dot(flo, rwh_ref[...]) + _dot(fb, rwl_ref[...]))
    _route(logits.T[0:N_EXPERTS], rb_ref, utri_ref, ltri_ref, ls_ref, wn_ref, seg_ref, tm)


def _mix0_kernel(oa_ref, ob_ref, x_ref, mod_ref, woa_ref, wob_ref, *rest, d, tm):
    y = _dot(oa_ref[...], woa_ref[...]) + _dot(ob_ref[...], wob_ref[...])
    _tail(y, x_ref[...], mod_ref[0], d, tm, *rest)


def _mix1_kernel(b_ref, z_ref, zp_ref, zn_ref, x_ref, mod_ref, cw_ref, wo_ref, *rest, d, tm, tiles_per_seq):
    i = pl.program_id(0)
    first = (i % tiles_per_seq) == 0
    last = (i % tiles_per_seq) == tiles_per_seq - 1
    z = z_ref[...].astype(F32)
    rowi = lax.broadcasted_iota(jnp.int32, (tm, 1), 0)
    zprev_row = jnp.where(first, 0.0, zp_ref[7:8, :].astype(F32))
    znext_row = jnp.where(last, 0.0, zn_ref[0:1, :].astype(F32))
    zm = jnp.where(rowi == 0, zprev_row, pltpu.roll(z, 1, 0))
    zp = jnp.where(rowi == tm - 1, znext_row, pltpu.roll(z, tm - 1, 0))
    conv = zm * cw_ref[0:1, :] + z * cw_ref[1:2, :] + zp * cw_ref[2:3, :]
    v = (b_ref[...].astype(F32) * conv).astype(BF16)
    _tail(_dot(v, wo_ref[...]), x_ref[...], mod_ref[0], d, tm, *rest)


def _tail_specs(t, d, tm, tail_w):
    full = lambda i: (0, 0)
    row = lambda i: (i, 0)
    col = lambda i: (0, i)
    in_specs = [pl.BlockSpec(w.shape, full) for w in tail_w]
    out_specs = [pl.BlockSpec((tm, d), row), pl.BlockSpec((tm, d), row), pl.BlockSpec((8, tm), col),
                 pl.BlockSpec((8, tm), col), pl.BlockSpec((1, N_EXPERTS, HEAD_LANES), lambda i: (i, 0, 0))]
    out_shape = [jax.ShapeDtypeStruct((t, d), BF16), jax.ShapeDtypeStruct((t, d), F32),
                 jax.ShapeDtypeStruct((8, t), jnp.int32), jax.ShapeDtypeStruct((8, t), F32),
                 jax.ShapeDtypeStruct((t // tm, N_EXPERTS, HEAD_LANES), F32)]
    return in_specs, out_specs, out_shape


def _mix0(oa, ob, x2, mod3, woa, wob, tail_w, *, tm, seq):
    t, d = x2.shape
    full = lambda i: (0, 0)
    row = lambda i: (i, 0)
    tin, out_specs, out_shape = _tail_specs(t, d, tm, tail_w)
    return pl.pallas_call(
        functools.partial(_mix0_kernel, d=d, tm=tm),
        grid=(t // tm,),
        in_specs=[pl.BlockSpec((tm, oa.shape[1]), row), pl.BlockSpec((tm, ob.shape[1]), row),
                  pl.BlockSpec((tm, d), row),
                  pl.BlockSpec((1, 1, 6 * d), lambda i: ((i * tm) // seq, 0, 0)),
                  pl.BlockSpec(woa.shape, full), pl.BlockSpec(wob.shape, full)] + tin,
        out_specs=out_specs,
        out_shape=out_shape,
        compiler_params=_cparams("arbitrary"),
        name="mix0_tail",
    )(oa, ob, x2, mod3, woa, wob, *tail_w)


def _mix1(bq, z, x2, mod3, cw, wo, tail_w, *, tm, seq):
    t, d = x2.shape
    full = lambda i: (0, 0)
    row = lambda i: (i, 0)
    hb = tm // 8
    nhb = t // 8
    tin, out_specs, out_shape = _tail_specs(t, d, tm, tail_w)
    return pl.pallas_call(
        functools.partial(_mix1_kernel, d=d, tm=tm, tiles_per_seq=seq // tm),
        grid=(t // tm,),
        in_specs=[pl.BlockSpec((tm, d), row), pl.BlockSpec((tm, d), row),
                  pl.BlockSpec((8, d), lambda i: (jnp.maximum(i * hb - 1, 0), 0)),
                  pl.BlockSpec((8, d), lambda i: (jnp.minimum((i + 1) * hb, nhb - 1), 0)),
                  pl.BlockSpec((tm, d), row),
                  pl.BlockSpec((1, 1, 6 * d), lambda i: ((i * tm) // seq, 0, 0)),
                  pl.BlockSpec(cw.shape, full), pl.BlockSpec(wo.shape, full)] + tin,
        out_specs=out_specs,
        out_shape=out_shape,
        compiler_params=_cparams("arbitrary"),
        name="mix1_tail",
    )(bq, z, z, z, x2, mod3, cw, wo, *tail_w)


def _inproj1_kernel(x_ref, mod_ref, nw_ref, win_ref, b_ref, z_ref, *, d):
    mod = mod_ref[0]
    a = (_rms(x_ref[...], nw_ref[...]) * (1.0 + mod[:, d:2 * d]) + mod[:, 0:d]).astype(BF16)
    u = _dot(a, win_ref[...])
    b_ref[...] = u[:, 0:d].astype(BF16)
    z_ref[...] = (u[:, d:2 * d] * u[:, 2 * d:3 * d]).astype(F32)


def _inproj1(x2, mod3, nw, win, *, tm, seq):
    t, d = x2.shape
    full = lambda i: (0, 0)
    row = lambda i: (i, 0)
    return pl.pallas_call(
        functools.partial(_inproj1_kernel, d=d),
        grid=(t // tm,),
        in_specs=[pl.BlockSpec((tm, d), row),
                  pl.BlockSpec((1, 1, 6 * d), lambda i: ((i * tm) // seq, 0, 0)),
                  pl.BlockSpec((1, d), full), pl.BlockSpec(win.shape, full)],
        out_specs=[pl.BlockSpec((tm, d), row), pl.BlockSpec((tm, d), row)],
        out_shape=[jax.ShapeDtypeStruct((t, d), BF16), jax.ShapeDtypeStruct((t, d), F32)],
        compiler_params=_cparams("arbitrary"),
        name="inproj1",
    )(x2, mod3, nw, win)


def _seg_copies(seg_ref, lst_ref, goff_ref, j, local, slot, glob, sem, to_global):
    def ebody(e, carry):
        n = seg_ref[j * N_EXPERTS + e]
        s0 = lst_ref[j * N_EXPERTS + e]
        d0 = goff_ref[j * N_EXPERTS + e]

        def gbody(g, c2):
            lrow = pl.multiple_of((s0 + g) * SEG_ROWS, SEG_ROWS)
            grow = pl.multiple_of((d0 + g) * SEG_ROWS, SEG_ROWS)
            lref = local.at[slot, pl.ds(lrow, SEG_ROWS)]
            gref = glob.at[pl.ds(grow, SEG_ROWS)]
            if to_global:
                pltpu.make_async_copy(lref, gref, sem.at[slot]).start()
            else:
                pltpu.make_async_copy(gref, lref, sem.at[slot]).start()
            return c2
        lax.fori_loop(0, n, gbody, 0)
        return carry
    lax.fori_loop(0, N_EXPERTS, ebody, 0)


def _seg_wait(n, local, slot, glob, sem):
    p = local.shape[1] // SEG_ROWS
    while p >= 1:
        @pl.when((n & p) != 0)
        def _(p=p):
            rows = p * SEG_ROWS
            pltpu.make_async_copy(glob.at[pl.ds(0, rows)], local.at[slot, pl.ds(0, rows)], sem.at[slot]).wait()
        p //= 2


def _sorted_onehot(ls_f, rc, vals):
    tm = ls_f.shape[1]
    riota = lax.broadcasted_iota(jnp.int32, (SORT_CHUNK, tm), 0).astype(F32).astype(BF16)
    rel = jnp.clip(ls_f - float(rc * SORT_CHUNK), -1.0, float(SORT_CHUNK)).astype(BF16)
    p = jnp.zeros((SORT_CHUNK, tm), BF16)
    for k in range(TOP_K):
        val = jnp.ones((1, tm), BF16) if vals is None else vals[k:k + 1, :]
        p = jnp.where(riota == rel[k:k + 1, :], val, p)
    return p


def _dispatch_kernel(seg_ref, lst_ref, goff_ref, ntot_ref, gap_ref, ls_ref, f_ref, xs_hbm, loc, zbuf, sem, zsem, *,
                     tm, lrows):
    j = pl.program_id(0)
    nj = pl.num_programs(0)
    slot = j % 2

    @pl.when(j >= 2)
    def _():
        _seg_wait(ntot_ref[j - 2], loc, slot, xs_hbm, sem)

    nrows = ntot_ref[j] * SEG_ROWS
    ls_f = ls_ref[...].astype(F32)
    def sort_chunk(rc):
        p = _sorted_onehot(ls_f, rc, None)
        loc[slot, rc * SORT_CHUNK:(rc + 1) * SORT_CHUNK, :] = _dot(p, f_ref[...]).astype(BF16)

    nfix = TOP_K * tm // SORT_CHUNK
    for rc in range(nfix):
        sort_chunk(rc)
    for rc in range(nfix, lrows // SORT_CHUNK):
        pl.when(rc * SORT_CHUNK < nrows)(functools.partial(sort_chunk, rc))

    _seg_copies(seg_ref, lst_ref, goff_ref, j, loc, slot, xs_hbm, sem, True)

    @pl.when(j == nj - 1)
    def _():
        zbuf[...] = jnp.zeros(zbuf.shape, BF16)
        zgran = zbuf.at[pl.ds(0, SEG_ROWS)]

        def zbody(e, carry):
            def gbody(g, c2):
                grow = pl.multiple_of((gap_ref[e] + g) * SEG_ROWS, SEG_ROWS)
                pltpu.make_async_copy(zgran, xs_hbm.at[pl.ds(grow, SEG_ROWS)], zsem.at[0]).start()
                return c2
            lax.fori_loop(0, gap_ref[N_EXPERTS + e], gbody, 0)
            return carry
        lax.fori_loop(0, N_EXPERTS, zbody, 0)

        def tbody(b, carry):
            brow = pl.multiple_of(b * ROUTE_BLOCK, ROUTE_BLOCK)
            pltpu.make_async_copy(zbuf, xs_hbm.at[pl.ds(brow, ROUTE_BLOCK)], zsem.at[1]).start()
            return carry
        nblocks = xs_hbm.shape[0] // ROUTE_BLOCK
        lax.fori_loop(gap_ref[2 * N_EXPERTS + 1], nblocks, tbody, 0)

        def zwait(i, carry):
            pltpu.make_async_copy(zgran, xs_hbm.at[pl.ds(0, SEG_ROWS)], zsem.at[0]).wait()
            return carry
        lax.fori_loop(0, gap_ref[2 * N_EXPERTS], zwait, 0)

        def twait(b, carry):
            pltpu.make_async_copy(zbuf, xs_hbm.at[pl.ds(0, ROUTE_BLOCK)], zsem.at[1]).wait()
            return carry
        lax.fori_loop(gap_ref[2 * N_EXPERTS + 1], nblocks, twait, 0)
        _seg_wait(ntot_ref[j], loc, slot, xs_hbm, sem)

        @pl.when(j >= 1)
        def _():
            _seg_wait(ntot_ref[j - 1], loc, 1 - slot, xs_hbm, sem)


def _dispatch(plan, ls, fb, *, tm, lrows, prows):
    t, d = fb.shape
    grid_spec = pltpu.PrefetchScalarGridSpec(
        num_scalar_prefetch=5,
        grid=(t // tm,),
        in_specs=[pl.BlockSpec((8, tm), lambda j, *_: (0, j)), pl.BlockSpec((tm, d), lambda j, *_: (j, 0))],
        out_specs=pl.BlockSpec(memory_space=pl.ANY),
        scratch_shapes=[pltpu.VMEM((2, lrows, d), BF16), pltpu.VMEM((ROUTE_BLOCK, d), BF16),
                        pltpu.SemaphoreType.DMA((2,)), pltpu.SemaphoreType.DMA((2,))],
    )
    return pl.pallas_call(
        functools.partial(_dispatch_kernel, tm=tm, lrows=lrows),
        grid_spec=grid_spec,
        out_shape=jax.ShapeDtypeStruct((prows, d), BF16),
        compiler_params=_cparams("arbitrary"),
        name="dispatch",
    )(plan["seg"], plan["lst"], plan["goff"], plan["ntot"], plan["gap"], ls, fb)


EXP_RING = 4


def _experts_kernel(bnd_ref, wg_ref, wu_ref, wd_ref, xs_hbm, ys_hbm, wgb, wub, wdb, xbuf, ybuf, xsem, ysem):
    e = pl.program_id(0)
    b0 = bnd_ref[e]
    b1 = bnd_ref[e + 1]
    nv = bnd_ref[N_EXPERTS]
    nblocks = xs_hbm.shape[0] // ROUTE_BLOCK

    def xcopy(b, slot):
        row = pl.multiple_of(b * ROUTE_BLOCK, ROUTE_BLOCK)
        return pltpu.make_async_copy(xs_hbm.at[pl.ds(row, ROUTE_BLOCK)], xbuf.at[slot], xsem.at[slot])

    def ycopy(b, slot):
        row = pl.multiple_of(b * ROUTE_BLOCK, ROUTE_BLOCK)
        return pltpu.make_async_copy(ybuf.at[slot], ys_hbm.at[pl.ds(row, ROUTE_BLOCK)], ysem.at[slot])

    @pl.when(e == 0)
    def _():
        for i in range(EXP_RING - 1):
            @pl.when(i < nv)
            def _(i=i):
                xcopy(i, i).start()

    @pl.when(b1 > b0)
    def _():
        wgb[...] = wg_ref[0, 0].astype(BF16)
        wub[...] = wu_ref[0, 0].astype(BF16)
        wdb[...] = wd_ref[0, 0].astype(BF16)

    def body(b, carry):
        slot = b % EXP_RING
        ahead = b + (EXP_RING - 1)

        @pl.when(ahead < nv)
        def _():
            xcopy(ahead, ahead % EXP_RING).start()

        xcopy(b, slot).wait()

        @pl.when(b >= EXP_RING)
        def _():
            ycopy(b - EXP_RING, slot).wait()

        x = xbuf[slot]
        act = (_silu(_dot(x, wgb[...])) * _dot(x, wub[...])).astype(BF16)
        ybuf[slot] = _dot(act, wdb[...]).astype(BF16)
        ycopy(b, slot).start()
        return carry
    lax.fori_loop(b0, b1, body, 0)

    @pl.when(e == N_EXPERTS - 1)
    def _():
        for i in range(1, EXP_RING + 1):
            @pl.when(nv >= i)
            def _(i=i):
                ycopy(nv - i, (nv - i) % EXP_RING).wait()

        ybuf[0] = jnp.zeros(ybuf.shape[1:], BF16)

        def tbody(b, carry):
            ycopy(b, 0).start()
            return carry
        lax.fori_loop(nv, nblocks, tbody, 0)

        def twait(b, carry):
            ycopy(b, 0).wait()
            return carry
        lax.fori_loop(nv, nblocks, twait, 0)


def _experts(bounds, xs, wg, wu, wd, layer):
    prows, d = xs.shape
    de = wg.shape[3]
    wsel = lambda e, bnd: (layer, e, 0, 0)
    grid_spec = pltpu.PrefetchScalarGridSpec(
        num_scalar_prefetch=1,
        grid=(N_EXPERTS,),
        in_specs=[pl.BlockSpec((1, 1, d, de), wsel), pl.BlockSpec((1, 1, d, de), wsel),
                  pl.BlockSpec((1, 1, de, d), wsel), pl.BlockSpec(memory_space=pl.ANY)],
        out_specs=pl.BlockSpec(memory_space=pl.ANY),
        scratch_shapes=[pltpu.VMEM((d, de), BF16), pltpu.VMEM((d, de), BF16), pltpu.VMEM((de, d), BF16),
                        pltpu.VMEM((EXP_RING, ROUTE_BLOCK, d), BF16), pltpu.VMEM((EXP_RING, ROUTE_BLOCK, d), BF16),
                        pltpu.SemaphoreType.DMA((EXP_RING,)), pltpu.SemaphoreType.DMA((EXP_RING,))],
    )
    return pl.pallas_call(
        _experts_kernel,
        grid_spec=grid_spec,
        out_shape=jax.ShapeDtypeStruct((prows, d), BF16),
        compiler_params=_cparams("arbitrary"),
        name="experts",
    )(bounds, wg, wu, wd, xs)


COMB_GROUP = 4


def _combine_kernel(seg_ref, lst_ref, goff_ref, ntot_ref, ls_ref, wn_ref, base_ref, mod_ref, fnw_ref, ys_hbm,
                    o_ref, loc, pw, acc, sem, *, tm, lrows, d, final):
    j = pl.program_id(0)
    nj = pl.num_programs(0)
    slot = j % 2

    @pl.when(j == 0)
    def _():
        loc[...] = jnp.zeros(loc.shape, BF16)
        _seg_copies(seg_ref, lst_ref, goff_ref, 0, loc, 0, ys_hbm, sem, False)

    @pl.when(j + 1 < nj)
    def _():
        _seg_copies(seg_ref, lst_ref, goff_ref, j + 1, loc, 1 - slot, ys_hbm, sem, False)

    ls_f = ls_ref[...].astype(F32)
    wn_b = wn_ref[...].astype(BF16)
    nfix = TOP_K * tm // SORT_CHUNK
    _seg_wait(ntot_ref[j], loc, slot, ys_hbm, sem)
    tn_dims = (((0,), (0,)), ((), ()))
    grp = COMB_GROUP * SORT_CHUNK
    total = None
    for g in range(nfix // COMB_GROUP):
        for rc in range(g * COMB_GROUP, (g + 1) * COMB_GROUP):
            pw[rc * SORT_CHUNK:(rc + 1) * SORT_CHUNK, :] = _sorted_onehot(ls_f, rc, wn_b)
        part = lax.dot_general(pw[g * grp:(g + 1) * grp, :], loc[slot, g * grp:(g + 1) * grp, :], tn_dims,
                               preferred_element_type=F32)
        total = part if total is None else total + part
    acc[...] = total

    nrows = ntot_ref[j] * SEG_ROWS
    for rc in range(nfix, lrows // SORT_CHUNK):
        @pl.when(rc * SORT_CHUNK < nrows)
        def _():
            ysc = loc[slot, rc * SORT_CHUNK:(rc + 1) * SORT_CHUNK, :]
            acc[...] += lax.dot_general(_sorted_onehot(ls_f, rc, wn_b), ysc, tn_dims, preferred_element_type=F32)

    h = base_ref[...] + mod_ref[0][:, 5 * d:6 * d] * acc[...]
    if final:
        h = _rms(h, fnw_ref[...])
    o_ref[...] = h


def _combine(plan, ls, wn, ys, base, mod3, fnw, *, tm, lrows, seq, final):
    t, d = base.shape
    full = lambda j, *_: (0, 0)
    row = lambda j, *_: (j, 0)
    col = lambda j, *_: (0, j)
    grid_spec = pltpu.PrefetchScalarGridSpec(
        num_scalar_prefetch=4,
        grid=(t // tm,),
        in_specs=[pl.BlockSpec((8, tm), col), pl.BlockSpec((8, tm), col), pl.BlockSpec((tm, d), row),
                  pl.BlockSpec((1, 1, 6 * d), lambda j, *_: ((j * tm) // seq, 0, 0)),
                  pl.BlockSpec((1, d), full), pl.BlockSpec(memory_space=pl.ANY)],
        out_specs=pl.BlockSpec((tm, d), row),
        scratch_shapes=[pltpu.VMEM((2, lrows, d), BF16), pltpu.VMEM((TOP_K * tm, tm), BF16),
                        pltpu.VMEM((tm, d), F32), pltpu.SemaphoreType.DMA((2,))],
    )
    return pl.pallas_call(
        functools.partial(_combine_kernel, tm=tm, lrows=lrows, d=d, final=final),
        grid_spec=grid_spec,
        out_shape=jax.ShapeDtypeStruct((t, d), F32),
        compiler_params=_cparams("arbitrary"),
        name="combine_final" if final else "combine",
    )(plan["seg"], plan["lst"], plan["goff"], plan["ntot"], ls, wn, base, mod3, fnw, ys)


def _dispatch_plan(seg_out):
    seg = seg_out[:, :, 0].astype(jnp.int32)
    gpb = ROUTE_BLOCK // SEG_ROWS
    lst = jnp.cumsum(seg, axis=1) - seg
    tile_off = jnp.cumsum(seg, axis=0) - seg
    tot = jnp.sum(seg, axis=0)
    region_blk = (tot + gpb - 1) // gpb
    gend_blk = jnp.cumsum(region_blk)
    gstart = (gend_blk - region_blk) * gpb
    goff = gstart[None, :] + tile_off
    gap_cnt = region_blk * gpb - tot
    gap = jnp.concatenate([gstart + tot, gap_cnt, jnp.sum(gap_cnt)[None], gend_blk[-1:]]).astype(jnp.int32)
    bounds = jnp.concatenate([jnp.zeros((1,), jnp.int32), gend_blk.astype(jnp.int32)])
    plan = {"seg": seg.reshape(-1), "lst": lst.reshape(-1).astype(jnp.int32),
            "goff": goff.reshape(-1).astype(jnp.int32), "ntot": jnp.sum(seg, axis=1).astype(jnp.int32), "gap": gap}
    return plan, bounds


def _rope_tables(n):
    rows = n // GRID_W
    row = jnp.repeat(jnp.arange(rows), GRID_W).astype(F32)
    col = jnp.tile(jnp.arange(GRID_W), rows).astype(F32)
    half = QK_ROPE // 2
    inv = 1.0 / (ROPE_BASE ** (jnp.arange(0, half, 2, dtype=F32) / half))
    ang = jnp.stack([row[:, None] * inv, col[:, None] * inv], axis=1)
    cos, sin = jnp.cos(ang), jnp.sin(ang)
    c32 = jnp.broadcast_to(cos[:, :, None, :], (n, 2, 2, half // 2)).reshape(n, QK_ROPE)
    s32 = jnp.broadcast_to(sin[:, :, None, :], (n, 2, 2, half // 2)).reshape(n, QK_ROPE)
    return c32, s32


def _swap_signed(w):
    wr = w.reshape(w.shape[:-1] + (2, 2, QK_ROPE // 4))
    return jnp.stack([-wr[..., 1, :], wr[..., 0, :]], axis=-2).reshape(w.shape)


def _head_group(parts, lead):
    width = sum(p.shape[-1] for p in parts)
    pad = jnp.zeros(lead + (N_HEADS, HEAD_LANES - width), F32)
    return jnp.concatenate(list(parts) + [pad], axis=-1).reshape(lead + (N_HEADS * HEAD_LANES,))


def kernel(x, c, ctx, c_ctx, ada_w, ada_b, norm_mix_w, norm_ffn_w, ev_w_in, ev_conv_w, ev_conv_b, ev_ln_w,
           ev_ln_b, ev_q_norm_w, ev_kv_norm_w, ev_w_uq, ev_w_ukv, ev_w_out, od_w_in, od_conv_w, od_w_out,
           router_w, router_bias, exp_w_gate, exp_w_up, exp_w_down, sh_w_gate, sh_w_up, sh_w_down,
           final_norm_w):
    bsz, n, d = x.shape
    nctx = ctx.shape[1]
    t = bsz * n
    tm = 512
    x2 = x.reshape(t, d)

    c16 = jnp.zeros((16, d), F32).at[0:bsz].set(c).at[bsz].set(c_ctx)
    mod = _ada(c16, ada_w, ada_b)
    mod0 = mod[0].reshape(16, 1, 6 * d)
    mod1 = mod[1].reshape(16, 1, 6 * d)

    w_in = ev_w_in[0]
    o_kr = 2 * CONV_CH + Q_LORA + KV_LORA
    w_kr = w_in[:, o_kr:o_kr + QK_ROPE]
    win_ext = jnp.concatenate(
        [w_in[:, :o_kr], w_kr, _swap_signed(w_kr), jnp.zeros((d, HEAD_LANES - 2 * QK_ROPE), F32)], axis=1).astype(BF16)
    wuq = ev_w_uq[0]
    wq_rope = wuq[..., QK_NOPE:]
    zq = jnp.zeros((Q_LORA, N_HEADS, QK_NOPE), F32)
    wq_ext = jnp.concatenate([_head_group([wuq], (Q_LORA,)),
                              _head_group([zq, _swap_signed(wq_rope)], (Q_LORA,))], axis=1).astype(BF16)
    wukv = ev_w_ukv[0]
    wkv_ext = jnp.concatenate([_head_group([wukv[..., :QK_NOPE]], (KV_LORA,)),
                               _head_group([wukv[..., QK_NOPE:]], (KV_LORA,))], axis=1).astype(BF16)
    eye = jnp.eye(QK_ROPE, dtype=F32)
    e_small = _head_group([jnp.zeros((QK_ROPE, N_HEADS, QK_NOPE), F32),
                           jnp.broadcast_to(eye[:, None, :], (QK_ROPE, N_HEADS, QK_ROPE))], (QK_ROPE,))
    emat = jnp.concatenate([e_small, jnp.zeros((HEAD_LANES - QK_ROPE, N_HEADS * HEAD_LANES), F32)], axis=0).astype(BF16)

    c32, s32 = _rope_tables(n)
    scale = (QK_NOPE + QK_ROPE) ** -0.5 * float(np.log2(np.e))
    zpad = jnp.zeros((n, HEAD_LANES - 2 * QK_ROPE), F32)
    ck_lat = jnp.concatenate([c32, s32, zpad], axis=1)
    zrope = jnp.zeros((n, HEAD_LANES - QK_NOPE - QK_ROPE), F32)
    cq_lat = jnp.concatenate([jnp.full((n, QK_NOPE), scale, F32), c32 * scale, zrope,
                              jnp.zeros((n, QK_NOPE), F32), s32 * scale, zrope], axis=1)
    ck_ctx = jnp.concatenate([jnp.ones((nctx, QK_ROPE), F32), jnp.zeros((nctx, HEAD_LANES - QK_ROPE), F32)], axis=1)
    cq_ctx = jnp.zeros((nctx, 2 * HEAD_LANES), F32)

    nw0 = norm_mix_w[0].reshape(1, d)
    qnw = ev_q_norm_w[0].reshape(1, Q_LORA)
    kvnw = ev_kv_norm_w[0].reshape(1, KV_LORA)
    glu, q, k_l, v_l = _inproj0(x2, mod0, lambda i: (i * tm) // n, nw0, win_ext, qnw, kvnw, wq_ext, wkv_ext,
                                emat, ck_lat, cq_lat, latent=True, tm=tm, seq=n)
    tmc = min(tm, nctx)
    k_c, v_c = _inproj0(ctx.reshape(bsz * nctx, d), mod0, lambda i: bsz, nw0,
                        win_ext[:, 2 * CONV_CH + Q_LORA:], qnw, kvnw, wq_ext, wkv_ext,
                        emat, ck_ctx, cq_ctx, latent=False, tm=tmc, seq=nctx)

    o_a = _conv(glu, ev_conv_w[0].reshape(CONV_WIDTH, CONV_CH), ev_conv_b[0].reshape(1, CONV_CH),
                ev_ln_w[0].reshape(1, CONV_CH), ev_ln_b[0].reshape(1, CONV_CH), tn=tm, seq=n)
    nk = N_HEADS * HEAD_LANES
    o_b = _attn(q.reshape(bsz, n, nk), k_c.reshape(bsz, nctx, nk), k_l.reshape(bsz, n, nk),
                v_c.reshape(bsz, nctx, nk), v_l.reshape(bsz, n, nk), tq=256)
    o_b = o_b.reshape(t, N_HEADS * V_HEAD)

    utri = (jnp.arange(tm)[:, None] < jnp.arange(tm)[None, :]).astype(BF16)
    ltri = (jnp.arange(N_EXPERTS)[None, :] < jnp.arange(N_EXPERTS)[:, None]).astype(BF16)

    def tail_weights(i):
        rw = jnp.concatenate([router_w[i], jnp.zeros((d, HEAD_LANES - N_EXPERTS), F32)], axis=1)
        rwh = rw.astype(BF16)
        rwl = (rw - rwh.astype(F32)).astype(BF16)
        return (norm_ffn_w[i].reshape(1, d), sh_w_gate[i].astype(BF16), sh_w_up[i].astype(BF16),
                sh_w_down[i].astype(BF16), rwh, rwl, router_bias[i].astype(F32).reshape(N_EXPERTS, 1), utri, ltri)

    ntiles = t // tm
    max_pad = SEG_ROWS - 1
    lrows = -(-(TOP_K * tm + N_EXPERTS * max_pad) // SORT_CHUNK) * SORT_CHUNK
    nblocks = -(-(TOP_K * t + ntiles * N_EXPERTS * max_pad + N_EXPERTS * (ROUTE_BLOCK - SEG_ROWS)) // ROUTE_BLOCK)
    fnw = final_norm_w.reshape(1, d)

    def moe(i, fb, base, ls, wn, seg_out, modi, final):
        plan, bounds = _dispatch_plan(seg_out)
        xs = _dispatch(plan, ls, fb, tm=tm, lrows=lrows, prows=nblocks * ROUTE_BLOCK)
        ys = _experts(bounds, xs, exp_w_gate, exp_w_up, exp_w_down, i)
        return _combine(plan, ls, wn, ys, base, modi, fnw, tm=tm, lrows=lrows, seq=n, final=final)

    w_out = ev_w_out[0].astype(BF16)
    tail0 = _mix0(o_a, o_b, x2, mod0, w_out[:CONV_CH], w_out[CONV_CH:], tail_weights(0), tm=tm, seq=n)
    h = moe(0, *tail0, mod0, False)

    bq, z = _inproj1(h, mod1, norm_mix_w[1].reshape(1, d), od_w_in[0].astype(BF16), tm=tm, seq=n)
    tail1 = _mix1(bq, z, h, mod1, od_conv_w[0].reshape(3, d), od_w_out[0].astype(BF16), tail_weights(1), tm=tm, seq=n)
    out = moe(1, *tail1, mod1, True)
    return out.reshape(bsz, n, d)
```

```python
import functools

import jax
import jax.numpy as jnp
import numpy as np
from jax import lax
from jax.experimental import pallas as pl
from jax.experimental.pallas import tpu as pltpu

F32 = jnp.float32
BF16 = jnp.bfloat16

EPS = 1e-6
GRID_W = 64
CONV_CH = 512
CONV_WIDTH = 31
N_HEADS = 8
QK_NOPE = 64
QK_ROPE = 32
V_HEAD = 64
Q_LORA = 256
KV_LORA = 128
ROPE_BASE = 10000.0
N_EXPERTS = 64
TOP_K = 6
N_GROUPS = 8
TOPK_GROUPS = 4
ROUTED_SCALE = 2.5
ROUTE_BLOCK = 512

HEAD_LANES = 128
SEG_ROWS = 16
SORT_CHUNK = 256
VMEM_LIMIT = 56 * 1024 * 1024


def _cparams(*sem, **kw):
    return pltpu.CompilerParams(dimension_semantics=sem, vmem_limit_bytes=VMEM_LIMIT, **kw)


def _dot(a, b):
    return jnp.dot(a, b, preferred_element_type=F32)


def _dot_nt(a, b):
    return lax.dot_general(a, b, (((1,), (1,)), ((), ())), preferred_element_type=F32)


def _rms(x, w):
    return x * lax.rsqrt(jnp.mean(x * x, axis=-1, keepdims=True) + EPS) * w


def _silu(x):
    return x * jax.nn.sigmoid(x)


def _ada_kernel(c_ref, w_ref, b_ref, o_ref):
    c = c_ref[...]
    o_ref[0] = _dot(_silu(c), w_ref[0]) + b_ref[0]


def _ada(c16, ada_w, ada_b):
    depth, d, n6 = ada_w.shape
    tn = 1536
    return pl.pallas_call(
        _ada_kernel,
        grid=(depth, n6 // tn),
        in_specs=[
            pl.BlockSpec((16, d), lambda l, j: (0, 0)),
            pl.BlockSpec((1, d, tn), lambda l, j: (l, 0, j)),
            pl.BlockSpec((1, 1, tn), lambda l, j: (l, 0, j)),
        ],
        out_specs=pl.BlockSpec((1, 16, tn), lambda l, j: (l, 0, j)),
        out_shape=jax.ShapeDtypeStruct((depth, 16, n6), F32),
        compiler_params=_cparams("arbitrary", "arbitrary"),
        name="ada",
    )(c16, ada_w, ada_b.reshape(depth, 1, n6))


def _inproj0_kernel(x_ref, mod_ref, nw_ref, win_ref, qnw_ref, kvnw_ref, wq_ref, wkv_ref, e_ref,
                    ck_ref, cq_ref, *outs, latent, d):
    x = x_ref[...]
    mod = mod_ref[0]
    a = (_rms(x, nw_ref[...]) * (1.0 + mod[:, d:2 * d]) + mod[:, 0:d]).astype(BF16)
    u = _dot(a, win_ref[...])
    if latent:
        glu_ref, q_ref, k_ref, v_ref = outs
        glu_ref[...] = u[:, 0:CONV_CH] * jax.nn.sigmoid(u[:, CONV_CH:2 * CONV_CH])
        o = 2 * CONV_CH
        cqn = _rms(u[:, o:o + Q_LORA], qnw_ref[...]).astype(BF16)
        o += Q_LORA
        cosq = cq_ref[:, 0:HEAD_LANES]
        sinq = cq_ref[:, HEAD_LANES:2 * HEAD_LANES]
        nq = N_HEADS * HEAD_LANES
        qa = _dot(cqn, wq_ref[:, 0:nq])
        qb = _dot(cqn, wq_ref[:, nq:2 * nq])
        for h in range(N_HEADS):
            sl = slice(h * HEAD_LANES, (h + 1) * HEAD_LANES)
            q_ref[:, sl] = (qa[:, sl] * cosq + qb[:, sl] * sinq).astype(BF16)
    else:
        k_ref, v_ref = outs
        o = 0
    nk = N_HEADS * HEAD_LANES
    kvn = _rms(u[:, o:o + KV_LORA], kvnw_ref[...]).astype(BF16)
    o += KV_LORA
    prod = u[:, o:o + HEAD_LANES] * ck_ref[...]
    krot = (prod + pltpu.roll(prod, HEAD_LANES - QK_ROPE, 1)).astype(BF16)
    k = _dot(kvn, wkv_ref[:, 0:nk]) + _dot(krot, e_ref[...])
    k_ref[...] = k.astype(BF16)
    lane = lax.broadcasted_iota(jnp.int32, (1, nk), 1)
    ones = jnp.where(lane % HEAD_LANES == V_HEAD, 1.0, 0.0).astype(F32)
    v_ref[...] = (_dot(kvn, wkv_ref[:, nk:2 * nk]) + ones).astype(BF16)


def _inproj0(x2, mod3, mod_row_fn, nw, win, qnw, kvnw, wq, wkv, emat, ck, cq, *, latent, tm, seq):
    t, d = x2.shape
    nk = N_HEADS * HEAD_LANES
    tiles_per_seq = seq // tm
    full = lambda i: (0, 0)
    in_specs = [
        pl.BlockSpec((tm, d), lambda i: (i, 0)),
        pl.BlockSpec((1, 1, 6 * d), lambda i: (mod_row_fn(i), 0, 0)),
        pl.BlockSpec((1, d), full),
        pl.BlockSpec(win.shape, full),
        pl.BlockSpec((1, Q_LORA), full),
        pl.BlockSpec((1, KV_LORA), full),
        pl.BlockSpec(wq.shape, full),
        pl.BlockSpec(wkv.shape, full),
        pl.BlockSpec(emat.shape, full),
        pl.BlockSpec((tm, HEAD_LANES), lambda i: (i % tiles_per_seq, 0)),
        pl.BlockSpec((tm, 2 * HEAD_LANES), lambda i: (i % tiles_per_seq, 0)),
    ]
    row = lambda i: (i, 0)
    out_specs = [pl.BlockSpec((tm, nk), row), pl.BlockSpec((tm, nk), row)]
    out_shape = [jax.ShapeDtypeStruct((t, nk), BF16), jax.ShapeDtypeStruct((t, nk), BF16)]
    if latent:
        out_specs = [pl.BlockSpec((tm, CONV_CH), row), pl.BlockSpec((tm, nk), row)] + out_specs
        out_shape = [jax.ShapeDtypeStruct((t, CONV_CH), F32), jax.ShapeDtypeStruct((t, nk), BF16)] + out_shape
    return pl.pallas_call(
        functools.partial(_inproj0_kernel, latent=latent, d=d),
        grid=(t // tm,),
        in_specs=in_specs,
        out_specs=out_specs,
        out_shape=out_shape,
        compiler_params=_cparams("arbitrary"),
        name="inproj0_lat" if latent else "inproj0_ctx",
    )(x2, mod3, nw, win, qnw, kvnw, wq, wkv, emat, ck, cq)


CONV_HALO = 16
CONV_ROWS = 64


def _conv_kernel(main_ref, prev_ref, next_ref, w_ref, b_ref, lnw_ref, lnb_ref, o_ref, win_ref, sh_ref, *,
                 tn, tiles_per_seq):
    i = pl.program_id(0)
    first = (i % tiles_per_seq) == 0
    last = (i % tiles_per_seq) == tiles_per_seq - 1
    win_ref[0:CONV_HALO, :] = jnp.where(first, 0.0, prev_ref[...])
    win_ref[CONV_HALO:CONV_HALO + tn, :] = main_ref[...]
    win_ref[CONV_HALO + tn:, :] = jnp.where(last, 0.0, next_ref[...])
    shl = sh_ref.shape[1]
    for r in range(1, 8):
        sh_ref[r - 1] = win_ref[r:r + shl, :]
    off0 = CONV_HALO - CONV_WIDTH // 2
    for c in range(tn // CONV_ROWS):
        acc = None
        for k in range(CONV_WIDTH):
            a, r = divmod(off0 + k, 8)
            lo = c * CONV_ROWS + 8 * a
            src = win_ref[lo:lo + CONV_ROWS, :] if r == 0 else sh_ref[r - 1, lo:lo + CONV_ROWS, :]
            term = src * w_ref[k:k + 1, :]
            acc = term if acc is None else acc + term
        y = acc + b_ref[...]
        mu = jnp.mean(y, axis=-1, keepdims=True)
        yc = y - mu
        var = jnp.mean(yc * yc, axis=-1, keepdims=True)
        yn = yc * lax.rsqrt(var + EPS) * lnw_ref[...] + lnb_ref[...]
        o_ref[c * CONV_ROWS:(c + 1) * CONV_ROWS, :] = _silu(yn).astype(BF16)


def _conv(glu, w, b, lnw, lnb, *, tn, seq):
    t, ch = glu.shape
    tiles_per_seq = seq // tn
    hb = tn // CONV_HALO
    nhb = t // CONV_HALO
    full = lambda i: (0, 0)
    return pl.pallas_call(
        functools.partial(_conv_kernel, tn=tn, tiles_per_seq=tiles_per_seq),
        grid=(t // tn,),
        in_specs=[
            pl.BlockSpec((tn, ch), lambda i: (i, 0)),
            pl.BlockSpec((CONV_HALO, ch), lambda i: (jnp.maximum(i * hb - 1, 0), 0)),
            pl.BlockSpec((CONV_HALO, ch), lambda i: (jnp.minimum((i + 1) * hb, nhb - 1), 0)),
            pl.BlockSpec(w.shape, full),
            pl.BlockSpec((1, ch), full),
            pl.BlockSpec((1, ch), full),
            pl.BlockSpec((1, ch), full),
        ],
        out_specs=pl.BlockSpec((tn, ch), lambda i: (i, 0)),
        out_shape=jax.ShapeDtypeStruct((t, ch), BF16),
        scratch_shapes=[pltpu.VMEM((tn + 2 * CONV_HALO, ch), F32),
                        pltpu.VMEM((7, tn + 2 * CONV_HALO - 8, ch), F32)],
        compiler_params=_cparams("arbitrary"),
        name="conv",
    )(glu, glu, glu, w, b, lnw, lnb)


ATT_KC = 512


ATT_HEADS = 4


def _attn_kernel(q_ref, kc_ref, kl_ref, vc_ref, vl_ref, o_ref, s_ref, *, nctx, nlat):
    nchunk = nlat // ATT_KC

    def keys(c):
        return (slice(0, nctx), slice(0, nctx)) if c < 0 else \
            (slice(c * ATT_KC, (c + 1) * ATT_KC), slice(nctx + c * ATT_KC, nctx + (c + 1) * ATT_KC))

    def score_chunk(h, c, m):
        sl = slice(h * HEAD_LANES, (h + 1) * HEAD_LANES)
        rows, cols = keys(c)
        k = kc_ref[0, rows, sl] if c < 0 else kl_ref[0, rows, sl]
        s = _dot_nt(q_ref[0, :, sl], k)
        s_ref[h % 2, :, cols] = s
        mc = jnp.max(s, axis=-1, keepdims=True)
        return mc if m is None else jnp.maximum(m, mc)

    def value_chunk(h, c, m, acc):
        sl = slice(h * HEAD_LANES, (h + 1) * HEAD_LANES)
        rows, cols = keys(c)
        v = vc_ref[0, rows, sl] if c < 0 else vl_ref[0, rows, sl]
        p = jnp.exp2((s_ref[h % 2, :, cols] - m).astype(BF16))
        pv = _dot(p, v)
        return pv if acc is None else acc + pv

    m = None
    for c in range(-1, nchunk):
        m = score_chunk(0, c, m)
    outs = []
    for h in range(ATT_HEADS):
        acc, m_next = None, None
        for c in range(-1, nchunk):
            acc = value_chunk(h, c, m, acc)
            if h + 1 < ATT_HEADS:
                m_next = score_chunk(h + 1, c, m_next)
        m = m_next
        outs.append(acc[:, 0:V_HEAD] / acc[:, V_HEAD:V_HEAD + 1])
    o_ref[0] = jnp.concatenate(outs, axis=-1).astype(BF16)


def _attn(q, kc, kl, vc, vl, *, tq):
    b, n, _ = q.shape
    nctx = kc.shape[1]
    hp = N_HEADS // ATT_HEADS
    w = ATT_HEADS * HEAD_LANES
    return pl.pallas_call(
        functools.partial(_attn_kernel, nctx=nctx, nlat=n),
        grid=(b, hp, n // tq),
        in_specs=[
            pl.BlockSpec((1, tq, w), lambda bi, j, i: (bi, i, j)),
            pl.BlockSpec((1, nctx, w), lambda bi, j, i: (bi, 0, j)),
            pl.BlockSpec((1, n, w), lambda bi, j, i: (bi, 0, j)),
            pl.BlockSpec((1, nctx, w), lambda bi, j, i: (bi, 0, j)),
            pl.BlockSpec((1, n, w), lambda bi, j, i: (bi, 0, j)),
        ],
        out_specs=pl.BlockSpec((1, tq, ATT_HEADS * V_HEAD), lambda bi, j, i: (bi, i, j)),
        out_shape=jax.ShapeDtypeStruct((b, n, N_HEADS * V_HEAD), BF16),
        scratch_shapes=[pltpu.VMEM((2, tq, nctx + n), F32)],
        compiler_params=_cparams("arbitrary", "arbitrary", "arbitrary"),
        name="attn",
    )(q, kc, kl, vc, vl)


def _route(lt, rb_ref, utri_ref, ltri_ref, ls_ref, wn_ref, seg_ref, tm):
    ninf = -jnp.inf
    gsz = N_EXPERTS // N_GROUPS
    s = jax.nn.sigmoid(lt)
    sel = s + rb_ref[...]
    sub = lax.broadcasted_iota(jnp.int32, (gsz, tm), 0).astype(F32)
    gs = []
    for g in range(N_GROUPS):
        xg = sel[g * gsz:(g + 1) * gsz]
        m1 = jnp.max(xg, axis=0, keepdims=True)
        i1 = jnp.min(jnp.where(xg == m1, sub, float(gsz)), axis=0, keepdims=True)
        m2 = jnp.max(jnp.where(sub == i1, ninf, xg), axis=0, keepdims=True)
        gs.append(m1 + m2)
    rows = []
    for g in range(N_GROUPS):
        beaten = jnp.zeros((1, tm), F32)
        for o in range(N_GROUPS):
            if o != g:
                beats = (gs[o] >= gs[g]) if o < g else (gs[o] > gs[g])
                beaten = beaten + jnp.where(beats, 1.0, 0.0)
        rows.append(jnp.where(beaten < float(TOPK_GROUPS), sel[g * gsz:(g + 1) * gsz], ninf))
    cur = jnp.concatenate(rows, axis=0)
    ei = lax.broadcasted_iota(jnp.int32, (N_EXPERTS, tm), 0).astype(F32)
    hits, ws = [], []
    for k in range(TOP_K):
        m = jnp.max(cur, axis=0, keepdims=True)
        idx = jnp.min(jnp.where(cur == m, ei, float(N_EXPERTS)), axis=0, keepdims=True)
        hit = ei == idx
        ws.append(jnp.sum(jnp.where(hit, s, 0.0), axis=0, keepdims=True))
        cur = jnp.where(hit, ninf, cur)
        hits.append(jnp.where(hit, 1.0, 0.0))
    wsum = ws[0]
    for k in range(1, TOP_K):
        wsum = wsum + ws[k]
    pre = _dot(jnp.concatenate(hits, axis=0).astype(BF16), utri_ref[...])
    tots = [jnp.sum(h, axis=1, keepdims=True) for h in hits]
    cnt = tots[0]
    for k in range(1, TOP_K):
        cnt = cnt + tots[k]
    seg = jnp.floor((cnt + (SEG_ROWS - 1.0)) * (1.0 / SEG_ROWS))
    segb = jnp.broadcast_to(seg, (N_EXPERTS, HEAD_LANES))
    seg_ref[0] = segb
    lstart = _dot(ltri_ref[...], segb.astype(BF16))[:, 0:1] * float(SEG_ROWS)
    basek = lstart
    for k in range(TOP_K):
        rank = jnp.sum(hits[k] * (pre[k * N_EXPERTS:(k + 1) * N_EXPERTS] + basek), axis=0, keepdims=True)
        ls_ref[k:k + 1, :] = rank.astype(jnp.int32)
        wn_ref[k:k + 1, :] = ws[k] / wsum * ROUTED_SCALE
        basek = basek + tots[k]
    ls_ref[TOP_K:8, :] = jnp.full((8 - TOP_K, tm), -1, jnp.int32)
    wn_ref[TOP_K:8, :] = jnp.zeros((8 - TOP_K, tm), F32)


def _tail(y, x, mod, d, tm, nfw_ref, swg_ref, swu_ref, swd_ref, rwh_ref, rwl_ref, rb_ref, utri_ref, ltri_ref,
          fb_ref, base_ref, ls_ref, wn_ref, seg_ref):
    g1 = mod[:, 2 * d:3 * d]
    sh2 = mod[:, 3 * d:4 * d]
    sc2 = mod[:, 4 * d:5 * d]
    g2 = mod[:, 5 * d:6 * d]
    h1 = x + g1 * y
    f = _rms(h1, nfw_ref[...]) * (1.0 + sc2) + sh2
    fb = f.astype(BF16)
    fb_ref[...] = fb
    act = (_silu(_dot(fb, swg_ref[...])) * _dot(fb, swu_ref[...])).astype(BF16)
    base_ref[...] = h1 + g2 * _dot(act, swd_ref[...])
    flo = (f - fb.astype(F32)).astype(BF16)
    logits = _dot(fb, rwh_ref[...]) + (_dot(flo, rwh_ref[...]) + _dot(fb, rwl_ref[...]))
    _route(logits.T[0:N_EXPERTS], rb_ref, utri_ref, ltri_ref, ls_ref, wn_ref, seg_ref, tm)


def _mix0_kernel(oa_ref, ob_ref, x_ref, mod_ref, woa_ref, wob_ref, *rest, d, tm):
    y = _dot(oa_ref[...], woa_ref[...]) + _dot(ob_ref[...], wob_ref[...])
    _tail(y, x_ref[...], mod_ref[0], d, tm, *rest)


def _mix1_kernel(b_ref, z_ref, zp_ref, zn_ref, x_ref, mod_ref, cw_ref, wo_ref, *rest, d, tm, tiles_per_seq):
    i = pl.program_id(0)
    first = (i % tiles_per_seq) == 0
    last = (i % tiles_per_seq) == tiles_per_seq - 1
    z = z_ref[...].astype(F32)
    rowi = lax.broadcasted_iota(jnp.int32, (tm, 1), 0)
    zprev_row = jnp.where(first, 0.0, zp_ref[7:8, :].astype(F32))
    znext_row = jnp.where(last, 0.0, zn_ref[0:1, :].astype(F32))
    zm = jnp.where(rowi == 0, zprev_row, pltpu.roll(z, 1, 0))
    zp = jnp.where(rowi == tm - 1, znext_row, pltpu.roll(z, tm - 1, 0))
    conv = zm * cw_ref[0:1, :] + z * cw_ref[1:2, :] + zp * cw_ref[2:3, :]
    v = (b_ref[...].astype(F32) * conv).astype(BF16)
    _tail(_dot(v, wo_ref[...]), x_ref[...], mod_ref[0], d, tm, *rest)


def _tail_specs(t, d, tm, tail_w):
    full = lambda i: (0, 0)
    row = lambda i: (i, 0)
    col = lambda i: (0, i)
    in_specs = [pl.BlockSpec(w.shape, full) for w in tail_w]
    out_specs = [pl.BlockSpec((tm, d), row), pl.BlockSpec((tm, d), row), pl.BlockSpec((8, tm), col),
                 pl.BlockSpec((8, tm), col), pl.BlockSpec((1, N_EXPERTS, HEAD_LANES), lambda i: (i, 0, 0))]
    out_shape = [jax.ShapeDtypeStruct((t, d), BF16), jax.ShapeDtypeStruct((t, d), F32),
                 jax.ShapeDtypeStruct((8, t), jnp.int32), jax.ShapeDtypeStruct((8, t), F32),
                 jax.ShapeDtypeStruct((t // tm, N_EXPERTS, HEAD_LANES), F32)]
    return in_specs, out_specs, out_shape


def _mix0(oa, ob, x2, mod3, woa, wob, tail_w, *, tm, seq):
    t, d = x2.shape
    full = lambda i: (0, 0)
    row = lambda i: (i, 0)
    tin, out_specs, out_shape = _tail_specs(t, d, tm, tail_w)
    return pl.pallas_call(
        functools.partial(_mix0_kernel, d=d, tm=tm),
        grid=(t // tm,),
        in_specs=[pl.BlockSpec((tm, oa.shape[1]), row), pl.BlockSpec((tm, ob.shape[1]), row),
                  pl.BlockSpec((tm, d), row),
                  pl.BlockSpec((1, 1, 6 * d), lambda i: ((i * tm) // seq, 0, 0)),
                  pl.BlockSpec(woa.shape, full), pl.BlockSpec(wob.shape, full)] + tin,
        out_specs=out_specs,
        out_shape=out_shape,
        compiler_params=_cparams("arbitrary"),
        name="mix0_tail",
    )(oa, ob, x2, mod3, woa, wob, *tail_w)


def _mix1(bq, z, x2, mod3, cw, wo, tail_w, *, tm, seq):
    t, d = x2.shape
    full = lambda i: (0, 0)
    row = lambda i: (i, 0)
    hb = tm // 8
    nhb = t // 8
    tin, out_specs, out_shape = _tail_specs(t, d, tm, tail_w)
    return pl.pallas_call(
        functools.partial(_mix1_kernel, d=d, tm=tm, tiles_per_seq=seq // tm),
        grid=(t // tm,),
        in_specs=[pl.BlockSpec((tm, d), row), pl.BlockSpec((tm, d), row),
                  pl.BlockSpec((8, d), lambda i: (jnp.maximum(i * hb - 1, 0), 0)),
                  pl.BlockSpec((8, d), lambda i: (jnp.minimum((i + 1) * hb, nhb - 1), 0)),
                  pl.BlockSpec((tm, d), row),
                  pl.BlockSpec((1, 1, 6 * d), lambda i: ((i * tm) // seq, 0, 0)),
                  pl.BlockSpec(cw.shape, full), pl.BlockSpec(wo.shape, full)] + tin,
        out_specs=out_specs,
        out_shape=out_shape,
        compiler_params=_cparams("arbitrary"),
        name="mix1_tail",
    )(bq, z, z, z, x2, mod3, cw, wo, *tail_w)


def _inproj1_kernel(x_ref, mod_ref, nw_ref, win_ref, b_ref, z_ref, *, d):
    mod = mod_ref[0]
    a = (_rms(x_ref[...], nw_ref[...]) * (1.0 + mod[:, d:2 * d]) + mod[:, 0:d]).astype(BF16)
    u = _dot(a, win_ref[...])
    b_ref[...] = u[:, 0:d].astype(BF16)
    z_ref[...] = (u[:, d:2 * d] * u[:, 2 * d:3 * d]).astype(F32)


def _inproj1(x2, mod3, nw, win, *, tm, seq):
    t, d = x2.shape
    full = lambda i: (0, 0)
    row = lambda i: (i, 0)
    return pl.pallas_call(
        functools.partial(_inproj1_kernel, d=d),
        grid=(t // tm,),
        in_specs=[pl.BlockSpec((tm, d), row),
                  pl.BlockSpec((1, 1, 6 * d), lambda i: ((i * tm) // seq, 0, 0)),
                  pl.BlockSpec((1, d), full), pl.BlockSpec(win.shape, full)],
        out_specs=[pl.BlockSpec((tm, d), row), pl.BlockSpec((tm, d), row)],
        out_shape=[jax.ShapeDtypeStruct((t, d), BF16), jax.ShapeDtypeStruct((t, d), F32)],
        compiler_params=_cparams("arbitrary"),
        name="inproj1",
    )(x2, mod3, nw, win)


def _seg_copies(dstg_ref, ntot_ref, j, local, slot, glob, sem, to_global):
    per_tile = local.shape[1] // SEG_ROWS

    def gbody(q, carry):
        lrow = pl.multiple_of(q * SEG_ROWS, SEG_ROWS)
        grow = pl.multiple_of(dstg_ref[j * per_tile + q] * SEG_ROWS, SEG_ROWS)
        lref = local.at[slot, pl.ds(lrow, SEG_ROWS)]
        gref = glob.at[pl.ds(grow, SEG_ROWS)]
        if to_global:
            pltpu.make_async_copy(lref, gref, sem.at[slot]).start()
        else:
            pltpu.make_async_copy(gref, lref, sem.at[slot]).start()
        return carry
    lax.fori_loop(0, ntot_ref[j], gbody, 0)


def _seg_wait(n, local, slot, glob, sem):
    p = local.shape[1] // SEG_ROWS
    while p >= 1:
        @pl.when((n & p) != 0)
        def _(p=p):
            rows = p * SEG_ROWS
            pltpu.make_async_copy(glob.at[pl.ds(0, rows)], local.at[slot, pl.ds(0, rows)], sem.at[slot]).wait()
        p //= 2


def _sorted_onehot(ls_f, rc, vals):
    tm = ls_f.shape[1]
    riota = lax.broadcasted_iota(jnp.int32, (SORT_CHUNK, tm), 0).astype(F32).astype(BF16)
    rel = jnp.clip(ls_f - float(rc * SORT_CHUNK), -1.0, float(SORT_CHUNK)).astype(BF16)
    p = jnp.zeros((SORT_CHUNK, tm), BF16)
    for k in range(TOP_K):
        val = jnp.ones((1, tm), BF16) if vals is None else vals[k:k + 1, :]
        p = jnp.where(riota == rel[k:k + 1, :], val, p)
    return p


def _dispatch_kernel(dstg_ref, ntot_ref, gap_ref, ls_ref, f_ref, xs_hbm, loc, zbuf, sem, zsem, *,
                     tm, lrows):
    j = pl.program_id(0)
    nj = pl.num_programs(0)
    slot = j % 2

    @pl.when(j >= 2)
    def _():
        _seg_wait(ntot_ref[j - 2], loc, slot, xs_hbm, sem)

    nrows = ntot_ref[j] * SEG_ROWS
    ls_f = ls_ref[...].astype(F32)
    def sort_chunk(rc):
        p = _sorted_onehot(ls_f, rc, None)
        loc[slot, rc * SORT_CHUNK:(rc + 1) * SORT_CHUNK, :] = _dot(p, f_ref[...]).astype(BF16)

    nfix = TOP_K * tm // SORT_CHUNK
    for rc in range(nfix):
        sort_chunk(rc)
    for rc in range(nfix, lrows // SORT_CHUNK):
        pl.when(rc * SORT_CHUNK < nrows)(functools.partial(sort_chunk, rc))

    _seg_copies(dstg_ref, ntot_ref, j, loc, slot, xs_hbm, sem, True)

    @pl.when(j == nj - 1)
    def _():
        zbuf[...] = jnp.zeros(zbuf.shape, BF16)
        zgran = zbuf.at[pl.ds(0, SEG_ROWS)]

        def zbody(e, carry):
            def gbody(g, c2):
                grow = pl.multiple_of((gap_ref[e] + g) * SEG_ROWS, SEG_ROWS)
                pltpu.make_async_copy(zgran, xs_hbm.at[pl.ds(grow, SEG_ROWS)], zsem.at[0]).start()
                return c2
            lax.fori_loop(0, gap_ref[N_EXPERTS + e], gbody, 0)
            return carry
        lax.fori_loop(0, N_EXPERTS, zbody, 0)

        def tbody(b, carry):
            brow = pl.multiple_of(b * ROUTE_BLOCK, ROUTE_BLOCK)
            pltpu.make_async_copy(zbuf, xs_hbm.at[pl.ds(brow, ROUTE_BLOCK)], zsem.at[1]).start()
            return carry
        nblocks = xs_hbm.shape[0] // ROUTE_BLOCK
        lax.fori_loop(gap_ref[2 * N_EXPERTS + 1], nblocks, tbody, 0)

        def zwait(i, carry):
            pltpu.make_async_copy(zgran, xs_hbm.at[pl.ds(0, SEG_ROWS)], zsem.at[0]).wait()
            return carry
        lax.fori_loop(0, gap_ref[2 * N_EXPERTS], zwait, 0)

        def twait(b, carry):
            pltpu.make_async_copy(zbuf, xs_hbm.at[pl.ds(0, ROUTE_BLOCK)], zsem.at[1]).wait()
            return carry
        lax.fori_loop(gap_ref[2 * N_EXPERTS + 1], nblocks, twait, 0)
        _seg_wait(ntot_ref[j], loc, slot, xs_hbm, sem)

        @pl.when(j >= 1)
        def _():
            _seg_wait(ntot_ref[j - 1], loc, 1 - slot, xs_hbm, sem)


def _dispatch(plan, ls, fb, *, tm, lrows, prows):
    t, d = fb.shape
    grid_spec = pltpu.PrefetchScalarGridSpec(
        num_scalar_prefetch=3,
        grid=(t // tm,),
        in_specs=[pl.BlockSpec((8, tm), lambda j, *_: (0, j)), pl.BlockSpec((tm, d), lambda j, *_: (j, 0))],
        out_specs=pl.BlockSpec(memory_space=pl.ANY),
        scratch_shapes=[pltpu.VMEM((2, lrows, d), BF16), pltpu.VMEM((ROUTE_BLOCK, d), BF16),
                        pltpu.SemaphoreType.DMA((2,)), pltpu.SemaphoreType.DMA((2,))],
    )
    return pl.pallas_call(
        functools.partial(_dispatch_kernel, tm=tm, lrows=lrows),
        grid_spec=grid_spec,
        out_shape=jax.ShapeDtypeStruct((prows, d), BF16),
        compiler_params=_cparams("arbitrary"),
        name="dispatch",
    )(plan["dstg"], plan["ntot"], plan["gap"], ls, fb)


EXP_RING = 4


def _experts_kernel(bnd_ref, wg_ref, wu_ref, wd_ref, xs_hbm, ys_hbm, wgb, wub, wdb, xbuf, ybuf, xsem, ysem):
    e = pl.program_id(0)
    b0 = bnd_ref[e]
    b1 = bnd_ref[e + 1]
    nv = bnd_ref[N_EXPERTS]
    nblocks = xs_hbm.shape[0] // ROUTE_BLOCK

    def xcopy(b, slot):
        row = pl.multiple_of(b * ROUTE_BLOCK, ROUTE_BLOCK)
        return pltpu.make_async_copy(xs_hbm.at[pl.ds(row, ROUTE_BLOCK)], xbuf.at[slot], xsem.at[slot])

    def ycopy(b, slot):
        row = pl.multiple_of(b * ROUTE_BLOCK, ROUTE_BLOCK)
        return pltpu.make_async_copy(ybuf.at[slot], ys_hbm.at[pl.ds(row, ROUTE_BLOCK)], ysem.at[slot])

    @pl.when(e == 0)
    def _():
        for i in range(EXP_RING - 1):
            @pl.when(i < nv)
            def _(i=i):
                xcopy(i, i).start()

    @pl.when(b1 > b0)
    def _():
        wgb[...] = wg_ref[0, 0].astype(BF16)
        wub[...] = wu_ref[0, 0].astype(BF16)
        wdb[...] = wd_ref[0, 0].astype(BF16)

    def body(b, carry):
        slot = b % EXP_RING
        ahead = b + (EXP_RING - 1)

        @pl.when(ahead < nv)
        def _():
            xcopy(ahead, ahead % EXP_RING).start()

        xcopy(b, slot).wait()

        @pl.when(b >= EXP_RING)
        def _():
            ycopy(b - EXP_RING, slot).wait()

        x = xbuf[slot]
        act = (_silu(_dot(x, wgb[...])) * _dot(x, wub[...])).astype(BF16)
        ybuf[slot] = _dot(act, wdb[...]).astype(BF16)
        ycopy(b, slot).start()
        return carry
    lax.fori_loop(b0, b1, body, 0)

    @pl.when(e == N_EXPERTS - 1)
    def _():
        for i in range(1, EXP_RING + 1):
            @pl.when(nv >= i)
            def _(i=i):
                ycopy(nv - i, (nv - i) % EXP_RING).wait()

        ybuf[0] = jnp.zeros(ybuf.shape[1:], BF16)

        def tbody(b, carry):
            ycopy(b, 0).start()
            return carry
        lax.fori_loop(nv, nblocks, tbody, 0)

        def twait(b, carry):
            ycopy(b, 0).wait()
            return carry
        lax.fori_loop(nv, nblocks, twait, 0)


def _experts(bounds, xs, wg, wu, wd, layer):
    prows, d = xs.shape
    de = wg.shape[3]
    wsel = lambda e, bnd: (layer, e, 0, 0)
    grid_spec = pltpu.PrefetchScalarGridSpec(
        num_scalar_prefetch=1,
        grid=(N_EXPERTS,),
        in_specs=[pl.BlockSpec((1, 1, d, de), wsel), pl.BlockSpec((1, 1, d, de), wsel),
                  pl.BlockSpec((1, 1, de, d), wsel), pl.BlockSpec(memory_space=pl.ANY)],
        out_specs=pl.BlockSpec(memory_space=pl.ANY),
        scratch_shapes=[pltpu.VMEM((d, de), BF16), pltpu.VMEM((d, de), BF16), pltpu.VMEM((de, d), BF16),
                        pltpu.VMEM((EXP_RING, ROUTE_BLOCK, d), BF16), pltpu.VMEM((EXP_RING, ROUTE_BLOCK, d), BF16),
                        pltpu.SemaphoreType.DMA((EXP_RING,)), pltpu.SemaphoreType.DMA((EXP_RING,))],
    )
    return pl.pallas_call(
        _experts_kernel,
        grid_spec=grid_spec,
        out_shape=jax.ShapeDtypeStruct((prows, d), BF16),
        compiler_params=_cparams("arbitrary"),
        name="experts",
    )(bounds, wg, wu, wd, xs)


COMB_GROUP = 4


def _combine_kernel(dstg_ref, ntot_ref, ls_ref, wn_ref, base_ref, mod_ref, fnw_ref, ys_hbm,
                    o_ref, loc, pw, acc, sem, *, tm, lrows, d, final):
    j = pl.program_id(0)
    nj = pl.num_programs(0)
    slot = j % 2

    @pl.when(j == 0)
    def _():
        loc[...] = jnp.zeros(loc.shape, BF16)
        _seg_copies(dstg_ref, ntot_ref, 0, loc, 0, ys_hbm, sem, False)

    @pl.when(j + 1 < nj)
    def _():
        _seg_copies(dstg_ref, ntot_ref, j + 1, loc, 1 - slot, ys_hbm, sem, False)

    ls_f = ls_ref[...].astype(F32)
    wn_b = wn_ref[...].astype(BF16)
    nfix = TOP_K * tm // SORT_CHUNK
    _seg_wait(ntot_ref[j], loc, slot, ys_hbm, sem)
    tn_dims = (((0,), (0,)), ((), ()))
    grp = COMB_GROUP * SORT_CHUNK
    total = None
    for g in range(nfix // COMB_GROUP):
        for rc in range(g * COMB_GROUP, (g + 1) * COMB_GROUP):
            pw[rc * SORT_CHUNK:(rc + 1) * SORT_CHUNK, :] = _sorted_onehot(ls_f, rc, wn_b)
        part = lax.dot_general(pw[g * grp:(g + 1) * grp, :], loc[slot, g * grp:(g + 1) * grp, :], tn_dims,
                               preferred_element_type=F32)
        total = part if total is None else total + part
    acc[...] = total

    nrows = ntot_ref[j] * SEG_ROWS
    for rc in range(nfix, lrows // SORT_CHUNK):
        @pl.when(rc * SORT_CHUNK < nrows)
        def _():
            ysc = loc[slot, rc * SORT_CHUNK:(rc + 1) * SORT_CHUNK, :]
            acc[...] += lax.dot_general(_sorted_onehot(ls_f, rc, wn_b), ysc, tn_dims, preferred_element_type=F32)

    h = base_ref[...] + mod_ref[0][:, 5 * d:6 * d] * acc[...]
    if final:
        h = _rms(h, fnw_ref[...])
    o_ref[...] = h


def _combine(plan, ls, wn, ys, base, mod3, fnw, *, tm, lrows, seq, final):
    t, d = base.shape
    full = lambda j, *_: (0, 0)
    row = lambda j, *_: (j, 0)
    col = lambda j, *_: (0, j)
    grid_spec = pltpu.PrefetchScalarGridSpec(
        num_scalar_prefetch=2,
        grid=(t // tm,),
        in_specs=[pl.BlockSpec((8, tm), col), pl.BlockSpec((8, tm), col), pl.BlockSpec((tm, d), row),
                  pl.BlockSpec((1, 1, 6 * d), lambda j, *_: ((j * tm) // seq, 0, 0)),
                  pl.BlockSpec((1, d), full), pl.BlockSpec(memory_space=pl.ANY)],
        out_specs=pl.BlockSpec((tm, d), row),
        scratch_shapes=[pltpu.VMEM((2, lrows, d), BF16), pltpu.VMEM((TOP_K * tm, tm), BF16),
                        pltpu.VMEM((tm, d), F32), pltpu.SemaphoreType.DMA((2,))],
    )
    return pl.pallas_call(
        functools.partial(_combine_kernel, tm=tm, lrows=lrows, d=d, final=final),
        grid_spec=grid_spec,
        out_shape=jax.ShapeDtypeStruct((t, d), F32),
        compiler_params=_cparams("arbitrary"),
        name="combine_final" if final else "combine",
    )(plan["dstg"], plan["ntot"], ls, wn, base, mod3, fnw, ys)


def _dispatch_plan(seg_out, per_tile):
    seg = seg_out[:, :, 0].astype(jnp.int32)
    gpb = ROUTE_BLOCK // SEG_ROWS
    lend = jnp.cumsum(seg, axis=1)
    lst = lend - seg
    tile_off = jnp.cumsum(seg, axis=0) - seg
    tot = jnp.sum(seg, axis=0)
    region_blk = (tot + gpb - 1) // gpb
    gend_blk = jnp.cumsum(region_blk)
    gstart = (gend_blk - region_blk) * gpb
    goff = gstart[None, :] + tile_off
    gap_cnt = region_blk * gpb - tot
    gap = jnp.concatenate([gstart + tot, gap_cnt, jnp.sum(gap_cnt)[None], gend_blk[-1:]]).astype(jnp.int32)
    bounds = jnp.concatenate([jnp.zeros((1,), jnp.int32), gend_blk.astype(jnp.int32)])
    q = jnp.arange(per_tile, dtype=jnp.int32)[None, :, None]
    owner = (q >= lst[:, None, :]) & (q < lend[:, None, :])
    dstg = jnp.sum(jnp.where(owner, (goff - lst)[:, None, :], 0), axis=-1) + q[:, :, 0]
    plan = {"dstg": dstg.reshape(-1).astype(jnp.int32), "ntot": lend[:, -1].astype(jnp.int32), "gap": gap}
    return plan, bounds


def _rope_tables(n):
    rows = n // GRID_W
    row = jnp.repeat(jnp.arange(rows), GRID_W).astype(F32)
    col = jnp.tile(jnp.arange(GRID_W), rows).astype(F32)
    half = QK_ROPE // 2
    inv = 1.0 / (ROPE_BASE ** (jnp.arange(0, half, 2, dtype=F32) / half))
    ang = jnp.stack([row[:, None] * inv, col[:, None] * inv], axis=1)
    cos, sin = jnp.cos(ang), jnp.sin(ang)
    c32 = jnp.broadcast_to(cos[:, :, None, :], (n, 2, 2, half // 2)).reshape(n, QK_ROPE)
    s32 = jnp.broadcast_to(sin[:, :, None, :], (n, 2, 2, half // 2)).reshape(n, QK_ROPE)
    return c32, s32


def _swap_signed(w):
    wr = w.reshape(w.shape[:-1] + (2, 2, QK_ROPE // 4))
    return jnp.stack([-wr[..., 1, :], wr[..., 0, :]], axis=-2).reshape(w.shape)


def _head_group(parts, lead):
    width = sum(p.shape[-1] for p in parts)
    pad = jnp.zeros(lead + (N_HEADS, HEAD_LANES - width), F32)
    return jnp.concatenate(list(parts) + [pad], axis=-1).reshape(lead + (N_HEADS * HEAD_LANES,))


def kernel(x, c, ctx, c_ctx, ada_w, ada_b, norm_mix_w, norm_ffn_w, ev_w_in, ev_conv_w, ev_conv_b, ev_ln_w,
           ev_ln_b, ev_q_norm_w, ev_kv_norm_w, ev_w_uq, ev_w_ukv, ev_w_out, od_w_in, od_conv_w, od_w_out,
           router_w, router_bias, exp_w_gate, exp_w_up, exp_w_down, sh_w_gate, sh_w_up, sh_w_down,
           final_norm_w):
    bsz, n, d = x.shape
    nctx = ctx.shape[1]
    t = bsz * n
    tm = 512
    x2 = x.reshape(t, d)

    c16 = jnp.zeros((16, d), F32).at[0:bsz].set(c).at[bsz].set(c_ctx)
    mod = _ada(c16, ada_w, ada_b)
    mod0 = mod[0].reshape(16, 1, 6 * d)
    mod1 = mod[1].reshape(16, 1, 6 * d)

    w_in = ev_w_in[0]
    o_kr = 2 * CONV_CH + Q_LORA + KV_LORA
    w_kr = w_in[:, o_kr:o_kr + QK_ROPE]
    win_ext = jnp.concatenate(
        [w_in[:, :o_kr], w_kr, _swap_signed(w_kr), jnp.zeros((d, HEAD_LANES - 2 * QK_ROPE), F32)], axis=1).astype(BF16)
    wuq = ev_w_uq[0]
    wq_rope = wuq[..., QK_NOPE:]
    zq = jnp.zeros((Q_LORA, N_HEADS, QK_NOPE), F32)
    wq_ext = jnp.concatenate([_head_group([wuq], (Q_LORA,)),
                              _head_group([zq, _swap_signed(wq_rope)], (Q_LORA,))], axis=1).astype(BF16)
    wukv = ev_w_ukv[0]
    wkv_ext = jnp.concatenate([_head_group([wukv[..., :QK_NOPE]], (KV_LORA,)),
                               _head_group([wukv[..., QK_NOPE:]], (KV_LORA,))], axis=1).astype(BF16)
    eye = jnp.eye(QK_ROPE, dtype=F32)
    e_small = _head_group([jnp.zeros((QK_ROPE, N_HEADS, QK_NOPE), F32),
                           jnp.broadcast_to(eye[:, None, :], (QK_ROPE, N_HEADS, QK_ROPE))], (QK_ROPE,))
    emat = jnp.concatenate([e_small, jnp.zeros((HEAD_LANES - QK_ROPE, N_HEADS * HEAD_LANES), F32)], axis=0).astype(BF16)

    c32, s32 = _rope_tables(n)
    scale = (QK_NOPE + QK_ROPE) ** -0.5 * float(np.log2(np.e))
    zpad = jnp.zeros((n, HEAD_LANES - 2 * QK_ROPE), F32)
    ck_lat = jnp.concatenate([c32, s32, zpad], axis=1)
    zrope = jnp.zeros((n, HEAD_LANES - QK_NOPE - QK_ROPE), F32)
    cq_lat = jnp.concatenate([jnp.full((n, QK_NOPE), scale, F32), c32 * scale, zrope,
                              jnp.zeros((n, QK_NOPE), F32), s32 * scale, zrope], axis=1)
    ck_ctx = jnp.concatenate([jnp.ones((nctx, QK_ROPE), F32), jnp.zeros((nctx, HEAD_LANES - QK_ROPE), F32)], axis=1)
    cq_ctx = jnp.zeros((nctx, 2 * HEAD_LANES), F32)

    nw0 = norm_mix_w[0].reshape(1, d)
    qnw = ev_q_norm_w[0].reshape(1, Q_LORA)
    kvnw = ev_kv_norm_w[0].reshape(1, KV_LORA)
    glu, q, k_l, v_l = _inproj0(x2, mod0, lambda i: (i * tm) // n, nw0, win_ext, qnw, kvnw, wq_ext, wkv_ext,
                                emat, ck_lat, cq_lat, latent=True, tm=tm, seq=n)
    tmc = min(tm, nctx)
    k_c, v_c = _inproj0(ctx.reshape(bsz * nctx, d), mod0, lambda i: bsz, nw0,
                        win_ext[:, 2 * CONV_CH + Q_LORA:], qnw, kvnw, wq_ext, wkv_ext,
                        emat, ck_ctx, cq_ctx, latent=False, tm=tmc, seq=nctx)

    o_a = _conv(glu, ev_conv_w[0].reshape(CONV_WIDTH, CONV_CH), ev_conv_b[0].reshape(1, CONV_CH),
                ev_ln_w[0].reshape(1, CONV_CH), ev_ln_b[0].reshape(1, CONV_CH), tn=tm, seq=n)
    nk = N_HEADS * HEAD_LANES
    o_b = _attn(q.reshape(bsz, n, nk), k_c.reshape(bsz, nctx, nk), k_l.reshape(bsz, n, nk),
                v_c.reshape(bsz, nctx, nk), v_l.reshape(bsz, n, nk), tq=256)
    o_b = o_b.reshape(t, N_HEADS * V_HEAD)

    utri = (jnp.arange(tm)[:, None] < jnp.arange(tm)[None, :]).astype(BF16)
    ltri = (jnp.arange(N_EXPERTS)[None, :] < jnp.arange(N_EXPERTS)[:, None]).astype(BF16)

    def tail_weights(i):
        rw = jnp.concatenate([router_w[i], jnp.zeros((d, HEAD_LANES - N_EXPERTS), F32)], axis=1)
        rwh = rw.astype(BF16)
        rwl = (rw - rwh.astype(F32)).astype(BF16)
        return (norm_ffn_w[i].reshape(1, d), sh_w_gate[i].astype(BF16), sh_w_up[i].astype(BF16),
                sh_w_down[i].astype(BF16), rwh, rwl, router_bias[i].astype(F32).reshape(N_EXPERTS, 1), utri, ltri)

    ntiles = t // tm
    max_pad = SEG_ROWS - 1
    lrows = -(-(TOP_K * tm + N_EXPERTS * max_pad) // SORT_CHUNK) * SORT_CHUNK
    nblocks = -(-(TOP_K * t + ntiles * N_EXPERTS * max_pad + N_EXPERTS * (ROUTE_BLOCK - SEG_ROWS)) // ROUTE_BLOCK)
    fnw = final_norm_w.reshape(1, d)

    def moe(i, fb, base, ls, wn, seg_out, modi, final):
        plan, bounds = _dispatch_plan(seg_out, lrows // SEG_ROWS)
        xs = _dispatch(plan, ls, fb, tm=tm, lrows=lrows, prows=nblocks * ROUTE_BLOCK)
        ys = _experts(bounds, xs, exp_w_gate, exp_w_up, exp_w_down, i)
        return _combine(plan, ls, wn, ys, base, modi, fnw, tm=tm, lrows=lrows, seq=n, final=final)

    w_out = ev_w_out[0].astype(BF16)
    tail0 = _mix0(o_a, o_b, x2, mod0, w_out[:CONV_CH], w_out[CONV_CH:], tail_weights(0), tm=tm, seq=n)
    h = moe(0, *tail0, mod0, False)

    bq, z = _inproj1(h, mod1, norm_mix_w[1].reshape(1, d), od_w_in[0].astype(BF16), tm=tm, seq=n)
    tail1 = _mix1(bq, z, h, mod1, od_conv_w[0].reshape(3, d), od_w_out[0].astype(BF16), tail_weights(1), tm=tm, seq=n)
    out = moe(1, *tail1, mod1, True)
    return out.reshape(bsz, n, d)
```

```python
import functools

import jax
import jax.numpy as jnp
import numpy as np
from jax import lax
from jax.experimental import pallas as pl
from jax.experimental.pallas import tpu as pltpu

F32 = jnp.float32
BF16 = jnp.bfloat16

EPS = 1e-6
GRID_W = 64
CONV_CH = 512
CONV_WIDTH = 31
N_HEADS = 8
QK_NOPE = 64
QK_ROPE = 32
V_HEAD = 64
Q_LORA = 256
KV_LORA = 128
ROPE_BASE = 10000.0
N_EXPERTS = 64
TOP_K = 6
N_GROUPS = 8
TOPK_GROUPS = 4
ROUTED_SCALE = 2.5
ROUTE_BLOCK = 512

HEAD_LANES = 128
SEG_ROWS = 16
SORT_CHUNK = 256
VMEM_LIMIT = 56 * 1024 * 1024


def _cparams(*sem, **kw):
    return pltpu.CompilerParams(dimension_semantics=sem, vmem_limit_bytes=VMEM_LIMIT, **kw)


def _dot(a, b):
    return jnp.dot(a, b, preferred_element_type=F32)


def _dot_nt(a, b):
    return lax.dot_general(a, b, (((1,), (1,)), ((), ())), preferred_element_type=F32)


def _rms(x, w):
    return x * lax.rsqrt(jnp.mean(x * x, axis=-1, keepdims=True) + EPS) * w


def _silu(x):
    return x * jax.nn.sigmoid(x)


def _ada_kernel(c_ref, w_ref, b_ref, o_ref):
    c = c_ref[...]
    o_ref[0] = _dot(_silu(c), w_ref[0]) + b_ref[0]


def _ada(c16, ada_w, ada_b):
    depth, d, n6 = ada_w.shape
    tn = 1536
    return pl.pallas_call(
        _ada_kernel,
        grid=(depth, n6 // tn),
        in_specs=[
            pl.BlockSpec((16, d), lambda l, j: (0, 0)),
            pl.BlockSpec((1, d, tn), lambda l, j: (l, 0, j)),
            pl.BlockSpec((1, 1, tn), lambda l, j: (l, 0, j)),
        ],
        out_specs=pl.BlockSpec((1, 16, tn), lambda l, j: (l, 0, j)),
        out_shape=jax.ShapeDtypeStruct((depth, 16, n6), F32),
        compiler_params=_cparams("arbitrary", "arbitrary"),
        name="ada",
    )(c16, ada_w, ada_b.reshape(depth, 1, n6))


def _inproj0_kernel(x_ref, mod_ref, nw_ref, win_ref, qnw_ref, kvnw_ref, wq_ref, wkv_ref, e_ref,
                    ck_ref, cq_ref, *outs, latent, d):
    x = x_ref[...]
    mod = mod_ref[0]
    a = (_rms(x, nw_ref[...]) * (1.0 + mod[:, d:2 * d]) + mod[:, 0:d]).astype(BF16)
    u = _dot(a, win_ref[...])
    if latent:
        glu_ref, q_ref, k_ref, v_ref = outs
        glu_ref[...] = u[:, 0:CONV_CH] * jax.nn.sigmoid(u[:, CONV_CH:2 * CONV_CH])
        o = 2 * CONV_CH
        cqn = _rms(u[:, o:o + Q_LORA], qnw_ref[...]).astype(BF16)
        o += Q_LORA
        cosq = cq_ref[:, 0:HEAD_LANES]
        sinq = cq_ref[:, HEAD_LANES:2 * HEAD_LANES]
        nq = N_HEADS * HEAD_LANES
        qa = _dot(cqn, wq_ref[:, 0:nq])
        qb = _dot(cqn, wq_ref[:, nq:2 * nq])
        for h in range(N_HEADS):
            sl = slice(h * HEAD_LANES, (h + 1) * HEAD_LANES)
            q_ref[:, sl] = (qa[:, sl] * cosq + qb[:, sl] * sinq).astype(BF16)
    else:
        k_ref, v_ref = outs
        o = 0
    nk = N_HEADS * HEAD_LANES
    kvn = _rms(u[:, o:o + KV_LORA], kvnw_ref[...]).astype(BF16)
    o += KV_LORA
    prod = u[:, o:o + HEAD_LANES] * ck_ref[...]
    krot = (prod + pltpu.roll(prod, HEAD_LANES - QK_ROPE, 1)).astype(BF16)
    k = _dot(kvn, wkv_ref[:, 0:nk]) + _dot(krot, e_ref[...])
    k_ref[...] = k.astype(BF16)
    lane = lax.broadcasted_iota(jnp.int32, (1, nk), 1)
    ones = jnp.where(lane % HEAD_LANES == V_HEAD, 1.0, 0.0).astype(F32)
    v_ref[...] = (_dot(kvn, wkv_ref[:, nk:2 * nk]) + ones).astype(BF16)


def _inproj0(x2, mod3, mod_row_fn, nw, win, qnw, kvnw, wq, wkv, emat, ck, cq, *, latent, tm, seq):
    t, d = x2.shape
    nk = N_HEADS * HEAD_LANES
    tiles_per_seq = seq // tm
    full = lambda i: (0, 0)
    in_specs = [
        pl.BlockSpec((tm, d), lambda i: (i, 0)),
        pl.BlockSpec((1, 1, 6 * d), lambda i: (mod_row_fn(i), 0, 0)),
        pl.BlockSpec((1, d), full),
        pl.BlockSpec(win.shape, full),
        pl.BlockSpec((1, Q_LORA), full),
        pl.BlockSpec((1, KV_LORA), full),
        pl.BlockSpec(wq.shape, full),
        pl.BlockSpec(wkv.shape, full),
        pl.BlockSpec(emat.shape, full),
        pl.BlockSpec((tm, HEAD_LANES), lambda i: (i % tiles_per_seq, 0)),
        pl.BlockSpec((tm, 2 * HEAD_LANES), lambda i: (i % tiles_per_seq, 0)),
    ]
    row = lambda i: (i, 0)
    out_specs = [pl.BlockSpec((tm, nk), row), pl.BlockSpec((tm, nk), row)]
    out_shape = [jax.ShapeDtypeStruct((t, nk), BF16), jax.ShapeDtypeStruct((t, nk), BF16)]
    if latent:
        out_specs = [pl.BlockSpec((tm, CONV_CH), row), pl.BlockSpec((tm, nk), row)] + out_specs
        out_shape = [jax.ShapeDtypeStruct((t, CONV_CH), F32), jax.ShapeDtypeStruct((t, nk), BF16)] + out_shape
    return pl.pallas_call(
        functools.partial(_inproj0_kernel, latent=latent, d=d),
        grid=(t // tm,),
        in_specs=in_specs,
        out_specs=out_specs,
        out_shape=out_shape,
        compiler_params=_cparams("arbitrary"),
        name="inproj0_lat" if latent else "inproj0_ctx",
    )(x2, mod3, nw, win, qnw, kvnw, wq, wkv, emat, ck, cq)


CONV_HALO = 16
CONV_ROWS = 64


def _conv_kernel(main_ref, prev_ref, next_ref, w_ref, b_ref, lnw_ref, lnb_ref, o_ref, win_ref, sh_ref, *,
                 tn, tiles_per_seq):
    i = pl.program_id(0)
    first = (i % tiles_per_seq) == 0
    last = (i % tiles_per_seq) == tiles_per_seq - 1
    win_ref[0:CONV_HALO, :] = jnp.where(first, 0.0, prev_ref[...])
    win_ref[CONV_HALO:CONV_HALO + tn, :] = main_ref[...]
    win_ref[CONV_HALO + tn:, :] = jnp.where(last, 0.0, next_ref[...])
    shl = sh_ref.shape[1]
    for r in range(1, 8):
        sh_ref[r - 1] = win_ref[r:r + shl, :]
    off0 = CONV_HALO - CONV_WIDTH // 2
    for c in range(tn // CONV_ROWS):
        acc = None
        for k in range(CONV_WIDTH):
            a, r = divmod(off0 + k, 8)
            lo = c * CONV_ROWS + 8 * a
            src = win_ref[lo:lo + CONV_ROWS, :] if r == 0 else sh_ref[r - 1, lo:lo + CONV_ROWS, :]
            term = src * w_ref[k:k + 1, :]
            acc = term if acc is None else acc + term
        y = acc + b_ref[...]
        mu = jnp.mean(y, axis=-1, keepdims=True)
        yc = y - mu
        var = jnp.mean(yc * yc, axis=-1, keepdims=True)
        yn = yc * lax.rsqrt(var + EPS) * lnw_ref[...] + lnb_ref[...]
        o_ref[c * CONV_ROWS:(c + 1) * CONV_ROWS, :] = _silu(yn).astype(BF16)


def _conv(glu, w, b, lnw, lnb, *, tn, seq):
    t, ch = glu.shape
    tiles_per_seq = seq // tn
    hb = tn // CONV_HALO
    nhb = t // CONV_HALO
    full = lambda i: (0, 0)
    return pl.pallas_call(
        functools.partial(_conv_kernel, tn=tn, tiles_per_seq=tiles_per_seq),
        grid=(t // tn,),
        in_specs=[
            pl.BlockSpec((tn, ch), lambda i: (i, 0)),
            pl.BlockSpec((CONV_HALO, ch), lambda i: (jnp.maximum(i * hb - 1, 0), 0)),
            pl.BlockSpec((CONV_HALO, ch), lambda i: (jnp.minimum((i + 1) * hb, nhb - 1), 0)),
            pl.BlockSpec(w.shape, full),
            pl.BlockSpec((1, ch), full),
            pl.BlockSpec((1, ch), full),
            pl.BlockSpec((1, ch), full),
        ],
        out_specs=pl.BlockSpec((tn, ch), lambda i: (i, 0)),
        out_shape=jax.ShapeDtypeStruct((t, ch), BF16),
        scratch_shapes=[pltpu.VMEM((tn + 2 * CONV_HALO, ch), F32),
                        pltpu.VMEM((7, tn + 2 * CONV_HALO - 8, ch), F32)],
        compiler_params=_cparams("arbitrary"),
        name="conv",
    )(glu, glu, glu, w, b, lnw, lnb)


ATT_KC = 512


ATT_HEADS = 4


def _attn_kernel(q_ref, kc_ref, kl_ref, vc_ref, vl_ref, o_ref, s_ref, *, nctx, nlat):
    nchunk = nlat // ATT_KC

    def keys(c):
        return (slice(0, nctx), slice(0, nctx)) if c < 0 else \
            (slice(c * ATT_KC, (c + 1) * ATT_KC), slice(nctx + c * ATT_KC, nctx + (c + 1) * ATT_KC))

    def score_chunk(h, c, m):
        sl = slice(h * HEAD_LANES, (h + 1) * HEAD_LANES)
        rows, cols = keys(c)
        k = kc_ref[0, rows, sl] if c < 0 else kl_ref[0, rows, sl]
        s = _dot_nt(q_ref[0, :, sl], k)
        s_ref[h % 2, :, cols] = s
        for i in range(s.shape[1] // HEAD_LANES):
            part = s[:, i * HEAD_LANES:(i + 1) * HEAD_LANES]
            m = part if m is None else jnp.maximum(m, part)
        return m

    def value_chunk(h, c, m, acc):
        sl = slice(h * HEAD_LANES, (h + 1) * HEAD_LANES)
        rows, cols = keys(c)
        v = vc_ref[0, rows, sl] if c < 0 else vl_ref[0, rows, sl]
        p = jnp.exp2((s_ref[h % 2, :, cols] - m).astype(BF16))
        pv = _dot(p, v)
        return pv if acc is None else acc + pv

    mp = None
    for c in range(-1, nchunk):
        mp = score_chunk(0, c, mp)
    outs = []
    for h in range(ATT_HEADS):
        m = jnp.max(mp, axis=-1, keepdims=True)
        acc, mp = None, None
        for c in range(-1, nchunk):
            acc = value_chunk(h, c, m, acc)
            if h + 1 < ATT_HEADS:
                mp = score_chunk(h + 1, c, mp)
        outs.append(acc[:, 0:V_HEAD] / acc[:, V_HEAD:V_HEAD + 1])
    o_ref[0] = jnp.concatenate(outs, axis=-1).astype(BF16)


def _attn(q, kc, kl, vc, vl, *, tq):
    b, n, _ = q.shape
    nctx = kc.shape[1]
    hp = N_HEADS // ATT_HEADS
    w = ATT_HEADS * HEAD_LANES
    return pl.pallas_call(
        functools.partial(_attn_kernel, nctx=nctx, nlat=n),
        grid=(b, hp, n // tq),
        in_specs=[
            pl.BlockSpec((1, tq, w), lambda bi, j, i: (bi, i, j)),
            pl.BlockSpec((1, nctx, w), lambda bi, j, i: (bi, 0, j)),
            pl.BlockSpec((1, n, w), lambda bi, j, i: (bi, 0, j)),
            pl.BlockSpec((1, nctx, w), lambda bi, j, i: (bi, 0, j)),
            pl.BlockSpec((1, n, w), lambda bi, j, i: (bi, 0, j)),
        ],
        out_specs=pl.BlockSpec((1, tq, ATT_HEADS * V_HEAD), lambda bi, j, i: (bi, i, j)),
        out_shape=jax.ShapeDtypeStruct((b, n, N_HEADS * V_HEAD), BF16),
        scratch_shapes=[pltpu.VMEM((2, tq, nctx + n), F32)],
        compiler_params=_cparams("arbitrary", "arbitrary", "arbitrary"),
        name="attn",
    )(q, kc, kl, vc, vl)


def _route(lt, rb_ref, utri_ref, ltri_ref, ls_ref, wn_ref, seg_ref, tm):
    ninf = -jnp.inf
    gsz = N_EXPERTS // N_GROUPS
    s = jax.nn.sigmoid(lt)
    sel = s + rb_ref[...]
    sub = lax.broadcasted_iota(jnp.int32, (gsz, tm), 0).astype(F32)
    gs = []
    for g in range(N_GROUPS):
        xg = sel[g * gsz:(g + 1) * gsz]
        m1 = jnp.max(xg, axis=0, keepdims=True)
        i1 = jnp.min(jnp.where(xg == m1, sub, float(gsz)), axis=0, keepdims=True)
        m2 = jnp.max(jnp.where(sub == i1, ninf, xg), axis=0, keepdims=True)
        gs.append(m1 + m2)
    rows = []
    for g in range(N_GROUPS):
        beaten = jnp.zeros((1, tm), F32)
        for o in range(N_GROUPS):
            if o != g:
                beats = (gs[o] >= gs[g]) if o < g else (gs[o] > gs[g])
                beaten = beaten + jnp.where(beats, 1.0, 0.0)
        rows.append(jnp.where(beaten < float(TOPK_GROUPS), sel[g * gsz:(g + 1) * gsz], ninf))
    cur = jnp.concatenate(rows, axis=0)
    ei = lax.broadcasted_iota(jnp.int32, (N_EXPERTS, tm), 0).astype(F32)
    hits, ws = [], []
    for k in range(TOP_K):
        m = jnp.max(cur, axis=0, keepdims=True)
        idx = jnp.min(jnp.where(cur == m, ei, float(N_EXPERTS)), axis=0, keepdims=True)
        hit = ei == idx
        ws.append(jnp.sum(jnp.where(hit, s, 0.0), axis=0, keepdims=True))
        cur = jnp.where(hit, ninf, cur)
        hits.append(jnp.where(hit, 1.0, 0.0))
    wsum = ws[0]
    for k in range(1, TOP_K):
        wsum = wsum + ws[k]
    pre = _dot(jnp.concatenate(hits, axis=0).astype(BF16), utri_ref[...])
    tots = [jnp.sum(h, axis=1, keepdims=True) for h in hits]
    cnt = tots[0]
    for k in range(1, TOP_K):
        cnt = cnt + tots[k]
    seg = jnp.floor((cnt + (SEG_ROWS - 1.0)) * (1.0 / SEG_ROWS))
    segb = jnp.broadcast_to(seg, (N_EXPERTS, HEAD_LANES))
    seg_ref[0] = segb
    lstart = _dot(ltri_ref[...], segb.astype(BF16))[:, 0:1] * float(SEG_ROWS)
    basek = lstart
    for k in range(TOP_K):
        rank = jnp.sum(hits[k] * (pre[k * N_EXPERTS:(k + 1) * N_EXPERTS] + basek), axis=0, keepdims=True)
        ls_ref[k:k + 1, :] = rank.astype(jnp.int32)
        wn_ref[k:k + 1, :] = ws[k] / wsum * ROUTED_SCALE
        basek = basek + tots[k]
    ls_ref[TOP_K:8, :] = jnp.full((8 - TOP_K, tm), -1, jnp.int32)
    wn_ref[TOP_K:8, :] = jnp.zeros((8 - TOP_K, tm), F32)


def _tail(y, x, mod, d, tm, nfw_ref, swg_ref, swu_ref, swd_ref, rwh_ref, rwl_ref, rb_ref, utri_ref, ltri_ref,
          fb_ref, base_ref, ls_ref, wn_ref, seg_ref):
    g1 = mod[:, 2 * d:3 * d]
    sh2 = mod[:, 3 * d:4 * d]
    sc2 = mod[:, 4 * d:5 * d]
    g2 = mod[:, 5 * d:6 * d]
    h1 = x + g1 * y
    f = _rms(h1, nfw_ref[...]) * (1.0 + sc2) + sh2
    fb = f.astype(BF16)
    fb_ref[...] = fb
    act = (_silu(_dot(fb, swg_ref[...])) * _dot(fb, swu_ref[...])).astype(BF16)
    base_ref[...] = h1 + g2 * _dot(act, swd_ref[...])
    flo = (f - fb.astype(F32)).astype(BF16)
    logits = _dot(fb, rwh_ref[...]) + (_dot(flo, rwh_ref[...]) + _dot(fb, rwl_ref[...]))
    _route(logits.T[0:N_EXPERTS], rb_ref, utri_ref, ltri_ref, ls_ref, wn_ref, seg_ref, tm)


def _mix0_kernel(oa_ref, ob_ref, x_ref, mod_ref, woa_ref, wob_ref, *rest, d, tm):
    y = _dot(oa_ref[...], woa_ref[...]) + _dot(ob_ref[...], wob_ref[...])
    _tail(y, x_ref[...], mod_ref[0], d, tm, *rest)


def _mix1_kernel(b_ref, z_ref, zp_ref, zn_ref, x_ref, mod_ref, cw_ref, wo_ref, *rest, d, tm, tiles_per_seq):
    i = pl.program_id(0)
    first = (i % tiles_per_seq) == 0
    last = (i % tiles_per_seq) == tiles_per_seq - 1
    z = z_ref[...].astype(F32)
    rowi = lax.broadcasted_iota(jnp.int32, (tm, 1), 0)
    zprev_row = jnp.where(first, 0.0, zp_ref[7:8, :].astype(F32))
    znext_row = jnp.where(last, 0.0, zn_ref[0:1, :].astype(F32))
    zm = jnp.where(rowi == 0, zprev_row, pltpu.roll(z, 1, 0))
    zp = jnp.where(rowi == tm - 1, znext_row, pltpu.roll(z, tm - 1, 0))
    conv = zm * cw_ref[0:1, :] + z * cw_ref[1:2, :] + zp * cw_ref[2:3, :]
    v = (b_ref[...].astype(F32) * conv).astype(BF16)
    _tail(_dot(v, wo_ref[...]), x_ref[...], mod_ref[0], d, tm, *rest)


def _tail_specs(t, d, tm, tail_w):
    full = lambda i: (0, 0)
    row = lambda i: (i, 0)
    col = lambda i: (0, i)
    in_specs = [pl.BlockSpec(w.shape, full) for w in tail_w]
    out_specs = [pl.BlockSpec((tm, d), row), pl.BlockSpec((tm, d), row), pl.BlockSpec((8, tm), col),
                 pl.BlockSpec((8, tm), col), pl.BlockSpec((1, N_EXPERTS, HEAD_LANES), lambda i: (i, 0, 0))]
    out_shape = [jax.ShapeDtypeStruct((t, d), BF16), jax.ShapeDtypeStruct((t, d), F32),
                 jax.ShapeDtypeStruct((8, t), jnp.int32), jax.ShapeDtypeStruct((8, t), F32),
                 jax.ShapeDtypeStruct((t // tm, N_EXPERTS, HEAD_LANES), F32)]
    return in_specs, out_specs, out_shape


def _mix0(oa, ob, x2, mod3, woa, wob, tail_w, *, tm, seq):
    t, d = x2.shape
    full = lambda i: (0, 0)
    row = lambda i: (i, 0)
    tin, out_specs, out_shape = _tail_specs(t, d, tm, tail_w)
    return pl.pallas_call(
        functools.partial(_mix0_kernel, d=d, tm=tm),
        grid=(t // tm,),
        in_specs=[pl.BlockSpec((tm, oa.shape[1]), row), pl.BlockSpec((tm, ob.shape[1]), row),
                  pl.BlockSpec((tm, d), row),
                  pl.BlockSpec((1, 1, 6 * d), lambda i: ((i * tm) // seq, 0, 0)),
                  pl.BlockSpec(woa.shape, full), pl.BlockSpec(wob.shape, full)] + tin,
        out_specs=out_specs,
        out_shape=out_shape,
        compiler_params=_cparams("arbitrary"),
        name="mix0_tail",
    )(oa, ob, x2, mod3, woa, wob, *tail_w)


def _mix1(bq, z, x2, mod3, cw, wo, tail_w, *, tm, seq):
    t, d = x2.shape
    full = lambda i: (0, 0)
    row = lambda i: (i, 0)
    hb = tm // 8
    nhb = t // 8
    tin, out_specs, out_shape = _tail_specs(t, d, tm, tail_w)
    return pl.pallas_call(
        functools.partial(_mix1_kernel, d=d, tm=tm, tiles_per_seq=seq // tm),
        grid=(t // tm,),
        in_specs=[pl.BlockSpec((tm, d), row), pl.BlockSpec((tm, d), row),
                  pl.BlockSpec((8, d), lambda i: (jnp.maximum(i * hb - 1, 0), 0)),
                  pl.BlockSpec((8, d), lambda i: (jnp.minimum((i + 1) * hb, nhb - 1), 0)),
                  pl.BlockSpec((tm, d), row),
                  pl.BlockSpec((1, 1, 6 * d), lambda i: ((i * tm) // seq, 0, 0)),
                  pl.BlockSpec(cw.shape, full), pl.BlockSpec(wo.shape, full)] + tin,
        out_specs=out_specs,
        out_shape=out_shape,
        compiler_params=_cparams("arbitrary"),
        name="mix1_tail",
    )(bq, z, z, z, x2, mod3, cw, wo, *tail_w)


def _inproj1_kernel(x_ref, mod_ref, nw_ref, win_ref, b_ref, z_ref, *, d):
    mod = mod_ref[0]
    a = (_rms(x_ref[...], nw_ref[...]) * (1.0 + mod[:, d:2 * d]) + mod[:, 0:d]).astype(BF16)
    u = _dot(a, win_ref[...])
    b_ref[...] = u[:, 0:d].astype(BF16)
    z_ref[...] = (u[:, d:2 * d] * u[:, 2 * d:3 * d]).astype(F32)


def _inproj1(x2, mod3, nw, win, *, tm, seq):
    t, d = x2.shape
    full = lambda i: (0, 0)
    row = lambda i: (i, 0)
    return pl.pallas_call(
        functools.partial(_inproj1_kernel, d=d),
        grid=(t // tm,),
        in_specs=[pl.BlockSpec((tm, d), row),
                  pl.BlockSpec((1, 1, 6 * d), lambda i: ((i * tm) // seq, 0, 0)),
                  pl.BlockSpec((1, d), full), pl.BlockSpec(win.shape, full)],
        out_specs=[pl.BlockSpec((tm, d), row), pl.BlockSpec((tm, d), row)],
        out_shape=[jax.ShapeDtypeStruct((t, d), BF16), jax.ShapeDtypeStruct((t, d), F32)],
        compiler_params=_cparams("arbitrary"),
        name="inproj1",
    )(x2, mod3, nw, win)


def _seg_copies(dstg_ref, ntot_ref, j, local, slot, glob, sem, to_global):
    per_tile = local.shape[1] // SEG_ROWS

    def gbody(q, carry):
        lrow = pl.multiple_of(q * SEG_ROWS, SEG_ROWS)
        grow = pl.multiple_of(dstg_ref[j * per_tile + q] * SEG_ROWS, SEG_ROWS)
        lref = local.at[slot, pl.ds(lrow, SEG_ROWS)]
        gref = glob.at[pl.ds(grow, SEG_ROWS)]
        if to_global:
            pltpu.make_async_copy(lref, gref, sem.at[slot]).start()
        else:
            pltpu.make_async_copy(gref, lref, sem.at[slot]).start()
        return carry
    lax.fori_loop(0, ntot_ref[j], gbody, 0)


def _seg_wait(n, local, slot, glob, sem):
    p = local.shape[1] // SEG_ROWS
    while p >= 1:
        @pl.when((n & p) != 0)
        def _(p=p):
            rows = p * SEG_ROWS
            pltpu.make_async_copy(glob.at[pl.ds(0, rows)], local.at[slot, pl.ds(0, rows)], sem.at[slot]).wait()
        p //= 2


def _sorted_onehot(ls_f, rc, vals):
    tm = ls_f.shape[1]
    riota = lax.broadcasted_iota(jnp.int32, (SORT_CHUNK, tm), 0).astype(F32).astype(BF16)
    rel = jnp.clip(ls_f - float(rc * SORT_CHUNK), -1.0, float(SORT_CHUNK)).astype(BF16)
    p = jnp.zeros((SORT_CHUNK, tm), BF16)
    for k in range(TOP_K):
        val = jnp.ones((1, tm), BF16) if vals is None else vals[k:k + 1, :]
        p = jnp.where(riota == rel[k:k + 1, :], val, p)
    return p


def _dispatch_kernel(dstg_ref, ntot_ref, gap_ref, ls_ref, f_ref, xs_hbm, loc, zbuf, sem, zsem, *,
                     tm, lrows):
    j = pl.program_id(0)
    nj = pl.num_programs(0)
    slot = j % 2

    @pl.when(j >= 2)
    def _():
        _seg_wait(ntot_ref[j - 2], loc, slot, xs_hbm, sem)

    nrows = ntot_ref[j] * SEG_ROWS
    ls_f = ls_ref[...].astype(F32)
    def sort_chunk(rc):
        p = _sorted_onehot(ls_f, rc, None)
        loc[slot, rc * SORT_CHUNK:(rc + 1) * SORT_CHUNK, :] = _dot(p, f_ref[...]).astype(BF16)

    nfix = TOP_K * tm // SORT_CHUNK
    for rc in range(nfix):
        sort_chunk(rc)
    for rc in range(nfix, lrows // SORT_CHUNK):
        pl.when(rc * SORT_CHUNK < nrows)(functools.partial(sort_chunk, rc))

    _seg_copies(dstg_ref, ntot_ref, j, loc, slot, xs_hbm, sem, True)

    @pl.when(j == nj - 1)
    def _():
        zbuf[...] = jnp.zeros(zbuf.shape, BF16)
        zgran = zbuf.at[pl.ds(0, SEG_ROWS)]

        def zbody(e, carry):
            def gbody(g, c2):
                grow = pl.multiple_of((gap_ref[e] + g) * SEG_ROWS, SEG_ROWS)
                pltpu.make_async_copy(zgran, xs_hbm.at[pl.ds(grow, SEG_ROWS)], zsem.at[0]).start()
                return c2
            lax.fori_loop(0, gap_ref[N_EXPERTS + e], gbody, 0)
            return carry
        lax.fori_loop(0, N_EXPERTS, zbody, 0)

        def tbody(b, carry):
            brow = pl.multiple_of(b * ROUTE_BLOCK, ROUTE_BLOCK)
            pltpu.make_async_copy(zbuf, xs_hbm.at[pl.ds(brow, ROUTE_BLOCK)], zsem.at[1]).start()
            return carry
        nblocks = xs_hbm.shape[0] // ROUTE_BLOCK
        lax.fori_loop(gap_ref[2 * N_EXPERTS + 1], nblocks, tbody, 0)

        def zwait(i, carry):
            pltpu.make_async_copy(zgran, xs_hbm.at[pl.ds(0, SEG_ROWS)], zsem.at[0]).wait()
            return carry
        lax.fori_loop(0, gap_ref[2 * N_EXPERTS], zwait, 0)

        def twait(b, carry):
            pltpu.make_async_copy(zbuf, xs_hbm.at[pl.ds(0, ROUTE_BLOCK)], zsem.at[1]).wait()
            return carry
        lax.fori_loop(gap_ref[2 * N_EXPERTS + 1], nblocks, twait, 0)
        _seg_wait(ntot_ref[j], loc, slot, xs_hbm, sem)

        @pl.when(j >= 1)
        def _():
            _seg_wait(ntot_ref[j - 1], loc, 1 - slot, xs_hbm, sem)


def _dispatch(plan, ls, fb, *, tm, lrows, prows):
    t, d = fb.shape
    grid_spec = pltpu.PrefetchScalarGridSpec(
        num_scalar_prefetch=3,
        grid=(t // tm,),
        in_specs=[pl.BlockSpec((8, tm), lambda j, *_: (0, j)), pl.BlockSpec((tm, d), lambda j, *_: (j, 0))],
        out_specs=pl.BlockSpec(memory_space=pl.ANY),
        scratch_shapes=[pltpu.VMEM((2, lrows, d), BF16), pltpu.VMEM((ROUTE_BLOCK, d), BF16),
                        pltpu.SemaphoreType.DMA((2,)), pltpu.SemaphoreType.DMA((2,))],
    )
    return pl.pallas_call(
        functools.partial(_dispatch_kernel, tm=tm, lrows=lrows),
        grid_spec=grid_spec,
        out_shape=jax.ShapeDtypeStruct((prows, d), BF16),
        compiler_params=_cparams("arbitrary"),
        name="dispatch",
    )(plan["dstg"], plan["ntot"], plan["gap"], ls, fb)


EXP_RING = 4


def _experts_kernel(bnd_ref, wg_ref, wu_ref, wd_ref, xs_hbm, ys_hbm, wgb, wub, wdb, xbuf, ybuf, xsem, ysem):
    e = pl.program_id(0)
    b0 = bnd_ref[e]
    b1 = bnd_ref[e + 1]
    nv = bnd_ref[N_EXPERTS]
    nblocks = xs_hbm.shape[0] // ROUTE_BLOCK

    def xcopy(b, slot):
        row = pl.multiple_of(b * ROUTE_BLOCK, ROUTE_BLOCK)
        return pltpu.make_async_copy(xs_hbm.at[pl.ds(row, ROUTE_BLOCK)], xbuf.at[slot], xsem.at[slot])

    def ycopy(b, slot):
        row = pl.multiple_of(b * ROUTE_BLOCK, ROUTE_BLOCK)
        return pltpu.make_async_copy(ybuf.at[slot], ys_hbm.at[pl.ds(row, ROUTE_BLOCK)], ysem.at[slot])

    @pl.when(e == 0)
    def _():
        for i in range(EXP_RING - 1):
            @pl.when(i < nv)
            def _(i=i):
                xcopy(i, i).start()

    @pl.when(b1 > b0)
    def _():
        wgb[...] = wg_ref[0, 0].astype(BF16)
        wub[...] = wu_ref[0, 0].astype(BF16)
        wdb[...] = wd_ref[0, 0].astype(BF16)

    def body(b, carry):
        slot = b % EXP_RING
        ahead = b + (EXP_RING - 1)

        @pl.when(ahead < nv)
        def _():
            xcopy(ahead, ahead % EXP_RING).start()

        xcopy(b, slot).wait()

        @pl.when(b >= EXP_RING)
        def _():
            ycopy(b - EXP_RING, slot).wait()

        x = xbuf[slot]
        act = (_silu(_dot(x, wgb[...])) * _dot(x, wub[...])).astype(BF16)
        ybuf[slot] = _dot(act, wdb[...]).astype(BF16)
        ycopy(b, slot).start()
        return carry
    lax.fori_loop(b0, b1, body, 0)

    @pl.when(e == N_EXPERTS - 1)
    def _():
        for i in range(1, EXP_RING + 1):
            @pl.when(nv >= i)
            def _(i=i):
                ycopy(nv - i, (nv - i) % EXP_RING).wait()

        ybuf[0] = jnp.zeros(ybuf.shape[1:], BF16)

        def tbody(b, carry):
            ycopy(b, 0).start()
            return carry
        lax.fori_loop(nv, nblocks, tbody, 0)

        def twait(b, carry):
            ycopy(b, 0).wait()
            return carry
        lax.fori_loop(nv, nblocks, twait, 0)


def _experts(bounds, xs, wg, wu, wd, layer):
    prows, d = xs.shape
    de = wg.shape[3]
    wsel = lambda e, bnd: (layer, e, 0, 0)
    grid_spec = pltpu.PrefetchScalarGridSpec(
        num_scalar_prefetch=1,
        grid=(N_EXPERTS,),
        in_specs=[pl.BlockSpec((1, 1, d, de), wsel), pl.BlockSpec((1, 1, d, de), wsel),
                  pl.BlockSpec((1, 1, de, d), wsel), pl.BlockSpec(memory_space=pl.ANY)],
        out_specs=pl.BlockSpec(memory_space=pl.ANY),
        scratch_shapes=[pltpu.VMEM((d, de), BF16), pltpu.VMEM((d, de), BF16), pltpu.VMEM((de, d), BF16),
                        pltpu.VMEM((EXP_RING, ROUTE_BLOCK, d), BF16), pltpu.VMEM((EXP_RING, ROUTE_BLOCK, d), BF16),
                        pltpu.SemaphoreType.DMA((EXP_RING,)), pltpu.SemaphoreType.DMA((EXP_RING,))],
    )
    return pl.pallas_call(
        _experts_kernel,
        grid_spec=grid_spec,
        out_shape=jax.ShapeDtypeStruct((prows, d), BF16),
        compiler_params=_cparams("arbitrary"),
        name="experts",
    )(bounds, wg, wu, wd, xs)


COMB_GROUP = 4


def _combine_kernel(dstg_ref, ntot_ref, ls_ref, wn_ref, base_ref, mod_ref, fnw_ref, ys_hbm,
                    o_ref, loc, pw, acc, sem, *, tm, lrows, d, final):
    j = pl.program_id(0)
    nj = pl.num_programs(0)
    slot = j % 2

    @pl.when(j == 0)
    def _():
        loc[...] = jnp.zeros(loc.shape, BF16)
        _seg_copies(dstg_ref, ntot_ref, 0, loc, 0, ys_hbm, sem, False)

    @pl.when(j + 1 < nj)
    def _():
        _seg_copies(dstg_ref, ntot_ref, j + 1, loc, 1 - slot, ys_hbm, sem, False)

    ls_f = ls_ref[...].astype(F32)
    wn_b = wn_ref[...].astype(BF16)
    nfix = TOP_K * tm // SORT_CHUNK
    _seg_wait(ntot_ref[j], loc, slot, ys_hbm, sem)
    tn_dims = (((0,), (0,)), ((), ()))
    grp = COMB_GROUP * SORT_CHUNK
    total = None
    for g in range(nfix // COMB_GROUP):
        for rc in range(g * COMB_GROUP, (g + 1) * COMB_GROUP):
            pw[rc * SORT_CHUNK:(rc + 1) * SORT_CHUNK, :] = _sorted_onehot(ls_f, rc, wn_b)
        part = lax.dot_general(pw[g * grp:(g + 1) * grp, :], loc[slot, g * grp:(g + 1) * grp, :], tn_dims,
                               preferred_element_type=F32)
        total = part if total is None else total + part
    acc[...] = total

    nrows = ntot_ref[j] * SEG_ROWS
    for rc in range(nfix, lrows // SORT_CHUNK):
        @pl.when(rc * SORT_CHUNK < nrows)
        def _():
            ysc = loc[slot, rc * SORT_CHUNK:(rc + 1) * SORT_CHUNK, :]
            acc[...] += lax.dot_general(_sorted_onehot(ls_f, rc, wn_b), ysc, tn_dims, preferred_element_type=F32)

    h = base_ref[...] + mod_ref[0][:, 5 * d:6 * d] * acc[...]
    if final:
        h = _rms(h, fnw_ref[...])
    o_ref[...] = h


def _combine(plan, ls, wn, ys, base, mod3, fnw, *, tm, lrows, seq, final):
    t, d = base.shape
    full = lambda j, *_: (0, 0)
    row = lambda j, *_: (j, 0)
    col = lambda j, *_: (0, j)
    grid_spec = pltpu.PrefetchScalarGridSpec(
        num_scalar_prefetch=2,
        grid=(t // tm,),
        in_specs=[pl.BlockSpec((8, tm), col), pl.BlockSpec((8, tm), col), pl.BlockSpec((tm, d), row),
                  pl.BlockSpec((1, 1, 6 * d), lambda j, *_: ((j * tm) // seq, 0, 0)),
                  pl.BlockSpec((1, d), full), pl.BlockSpec(memory_space=pl.ANY)],
        out_specs=pl.BlockSpec((tm, d), row),
        scratch_shapes=[pltpu.VMEM((2, lrows, d), BF16), pltpu.VMEM((TOP_K * tm, tm), BF16),
                        pltpu.VMEM((tm, d), F32), pltpu.SemaphoreType.DMA((2,))],
    )
    return pl.pallas_call(
        functools.partial(_combine_kernel, tm=tm, lrows=lrows, d=d, final=final),
        grid_spec=grid_spec,
        out_shape=jax.ShapeDtypeStruct((t, d), F32),
        compiler_params=_cparams("arbitrary"),
        name="combine_final" if final else "combine",
    )(plan["dstg"], plan["ntot"], ls, wn, base, mod3, fnw, ys)


def _dispatch_plan(seg_out, per_tile):
    seg = seg_out[:, :, 0].astype(jnp.int32)
    gpb = ROUTE_BLOCK // SEG_ROWS
    lend = jnp.cumsum(seg, axis=1)
    lst = lend - seg
    tile_off = jnp.cumsum(seg, axis=0) - seg
    tot = jnp.sum(seg, axis=0)
    region_blk = (tot + gpb - 1) // gpb
    gend_blk = jnp.cumsum(region_blk)
    gstart = (gend_blk - region_blk) * gpb
    goff = gstart[None, :] + tile_off
    gap_cnt = region_blk * gpb - tot
    gap = jnp.concatenate([gstart + tot, gap_cnt, jnp.sum(gap_cnt)[None], gend_blk[-1:]]).astype(jnp.int32)
    bounds = jnp.concatenate([jnp.zeros((1,), jnp.int32), gend_blk.astype(jnp.int32)])
    q = jnp.arange(per_tile, dtype=jnp.int32)[None, :, None]
    owner = (q >= lst[:, None, :]) & (q < lend[:, None, :])
    dstg = jnp.sum(jnp.where(owner, (goff - lst)[:, None, :], 0), axis=-1) + q[:, :, 0]
    plan = {"dstg": dstg.reshape(-1).astype(jnp.int32), "ntot": lend[:, -1].astype(jnp.int32), "gap": gap}
    return plan, bounds


def _rope_tables(n):
    rows = n // GRID_W
    row = jnp.repeat(jnp.arange(rows), GRID_W).astype(F32)
    col = jnp.tile(jnp.arange(GRID_W), rows).astype(F32)
    half = QK_ROPE // 2
    inv = 1.0 / (ROPE_BASE ** (jnp.arange(0, half, 2, dtype=F32) / half))
    ang = jnp.stack([row[:, None] * inv, col[:, None] * inv], axis=1)
    cos, sin = jnp.cos(ang), jnp.sin(ang)
    c32 = jnp.broadcast_to(cos[:, :, None, :], (n, 2, 2, half // 2)).reshape(n, QK_ROPE)
    s32 = jnp.broadcast_to(sin[:, :, None, :], (n, 2, 2, half // 2)).reshape(n, QK_ROPE)
    return c32, s32


def _swap_signed(w):
    wr = w.reshape(w.shape[:-1] + (2, 2, QK_ROPE // 4))
    return jnp.stack([-wr[..., 1, :], wr[..., 0, :]], axis=-2).reshape(w.shape)


def _head_group(parts, lead):
    width = sum(p.shape[-1] for p in parts)
    pad = jnp.zeros(lead + (N_HEADS, HEAD_LANES - width), F32)
    return jnp.concatenate(list(parts) + [pad], axis=-1).reshape(lead + (N_HEADS * HEAD_LANES,))


def kernel(x, c, ctx, c_ctx, ada_w, ada_b, norm_mix_w, norm_ffn_w, ev_w_in, ev_conv_w, ev_conv_b, ev_ln_w,
           ev_ln_b, ev_q_norm_w, ev_kv_norm_w, ev_w_uq, ev_w_ukv, ev_w_out, od_w_in, od_conv_w, od_w_out,
           router_w, router_bias, exp_w_gate, exp_w_up, exp_w_down, sh_w_gate, sh_w_up, sh_w_down,
           final_norm_w):
    bsz, n, d = x.shape
    nctx = ctx.shape[1]
    t = bsz * n
    tm = 512
    x2 = x.reshape(t, d)

    c16 = jnp.zeros((16, d), F32).at[0:bsz].set(c).at[bsz].set(c_ctx)
    mod = _ada(c16, ada_w, ada_b)
    mod0 = mod[0].reshape(16, 1, 6 * d)
    mod1 = mod[1].reshape(16, 1, 6 * d)

    w_in = ev_w_in[0]
    o_kr = 2 * CONV_CH + Q_LORA + KV_LORA
    w_kr = w_in[:, o_kr:o_kr + QK_ROPE]
    win_ext = jnp.concatenate(
        [w_in[:, :o_kr], w_kr, _swap_signed(w_kr), jnp.zeros((d, HEAD_LANES - 2 * QK_ROPE), F32)], axis=1).astype(BF16)
    wuq = ev_w_uq[0]
    wq_rope = wuq[..., QK_NOPE:]
    zq = jnp.zeros((Q_LORA, N_HEADS, QK_NOPE), F32)
    wq_ext = jnp.concatenate([_head_group([wuq], (Q_LORA,)),
                              _head_group([zq, _swap_signed(wq_rope)], (Q_LORA,))], axis=1).astype(BF16)
    wukv = ev_w_ukv[0]
    wkv_ext = jnp.concatenate([_head_group([wukv[..., :QK_NOPE]], (KV_LORA,)),
                               _head_group([wukv[..., QK_NOPE:]], (KV_LORA,))], axis=1).astype(BF16)
    eye = jnp.eye(QK_ROPE, dtype=F32)
    e_small = _head_group([jnp.zeros((QK_ROPE, N_HEADS, QK_NOPE), F32),
                           jnp.broadcast_to(eye[:, None, :], (QK_ROPE, N_HEADS, QK_ROPE))], (QK_ROPE,))
    emat = jnp.concatenate([e_small, jnp.zeros((HEAD_LANES - QK_ROPE, N_HEADS * HEAD_LANES), F32)], axis=0).astype(BF16)

    c32, s32 = _rope_tables(n)
    scale = (QK_NOPE + QK_ROPE) ** -0.5 * float(np.log2(np.e))
    zpad = jnp.zeros((n, HEAD_LANES - 2 * QK_ROPE), F32)
    ck_lat = jnp.concatenate([c32, s32, zpad], axis=1)
    zrope = jnp.zeros((n, HEAD_LANES - QK_NOPE - QK_ROPE), F32)
    cq_lat = jnp.concatenate([jnp.full((n, QK_NOPE), scale, F32), c32 * scale, zrope,
                              jnp.zeros((n, QK_NOPE), F32), s32 * scale, zrope], axis=1)
    ck_ctx = jnp.concatenate([jnp.ones((nctx, QK_ROPE), F32), jnp.zeros((nctx, HEAD_LANES - QK_ROPE), F32)], axis=1)
    cq_ctx = jnp.zeros((nctx, 2 * HEAD_LANES), F32)

    nw0 = norm_mix_w[0].reshape(1, d)
    qnw = ev_q_norm_w[0].reshape(1, Q_LORA)
    kvnw = ev_kv_norm_w[0].reshape(1, KV_LORA)
    glu, q, k_l, v_l = _inproj0(x2, mod0, lambda i: (i * tm) // n, nw0, win_ext, qnw, kvnw, wq_ext, wkv_ext,
                                emat, ck_lat, cq_lat, latent=True, tm=tm, seq=n)
    tmc = min(tm, nctx)
    k_c, v_c = _inproj0(ctx.reshape(bsz * nctx, d), mod0, lambda i: bsz, nw0,
                        win_ext[:, 2 * CONV_CH + Q_LORA:], qnw, kvnw, wq_ext, wkv_ext,
                        emat, ck_ctx, cq_ctx, latent=False, tm=tmc, seq=nctx)

    o_a = _conv(glu, ev_conv_w[0].reshape(CONV_WIDTH, CONV_CH), ev_conv_b[0].reshape(1, CONV_CH),
                ev_ln_w[0].reshape(1, CONV_CH), ev_ln_b[0].reshape(1, CONV_CH), tn=tm, seq=n)
    nk = N_HEADS * HEAD_LANES
    o_b = _attn(q.reshape(bsz, n, nk), k_c.reshape(bsz, nctx, nk), k_l.reshape(bsz, n, nk),
                v_c.reshape(bsz, nctx, nk), v_l.reshape(bsz, n, nk), tq=256)
    o_b = o_b.reshape(t, N_HEADS * V_HEAD)

    utri = (jnp.arange(tm)[:, None] < jnp.arange(tm)[None, :]).astype(BF16)
    ltri = (jnp.arange(N_EXPERTS)[None, :] < jnp.arange(N_EXPERTS)[:, None]).astype(BF16)

    def tail_weights(i):
        rw = jnp.concatenate([router_w[i], jnp.zeros((d, HEAD_LANES - N_EXPERTS), F32)], axis=1)
        rwh = rw.astype(BF16)
        rwl = (rw - rwh.astype(F32)).astype(BF16)
        return (norm_ffn_w[i].reshape(1, d), sh_w_gate[i].astype(BF16), sh_w_up[i].astype(BF16),
                sh_w_down[i].astype(BF16), rwh, rwl, router_bias[i].astype(F32).reshape(N_EXPERTS, 1), utri, ltri)

    ntiles = t // tm
    max_pad = SEG_ROWS - 1
    lrows = -(-(TOP_K * tm + N_EXPERTS * max_pad) // SORT_CHUNK) * SORT_CHUNK
    nblocks = -(-(TOP_K * t + ntiles * N_EXPERTS * max_pad + N_EXPERTS * (ROUTE_BLOCK - SEG_ROWS)) // ROUTE_BLOCK)
    fnw = final_norm_w.reshape(1, d)

    def moe(i, fb, base, ls, wn, seg_out, modi, final):
        plan, bounds = _dispatch_plan(seg_out, lrows // SEG_ROWS)
        xs = _dispatch(plan, ls, fb, tm=tm, lrows=lrows, prows=nblocks * ROUTE_BLOCK)
        ys = _experts(bounds, xs, exp_w_gate, exp_w_up, exp_w_down, i)
        return _combine(plan, ls, wn, ys, base, modi, fnw, tm=tm, lrows=lrows, seq=n, final=final)

    w_out = ev_w_out[0].astype(BF16)
    tail0 = _mix0(o_a, o_b, x2, mod0, w_out[:CONV_CH], w_out[CONV_CH:], tail_weights(0), tm=tm, seq=n)
    h = moe(0, *tail0, mod0, False)

    bq, z = _inproj1(h, mod1, norm_mix_w[1].reshape(1, d), od_w_in[0].astype(BF16), tm=tm, seq=n)
    tail1 = _mix1(bq, z, h, mod1, od_conv_w[0].reshape(3, d), od_w_out[0].astype(BF16), tail_weights(1), tm=tm, seq=n)
    out = moe(1, *tail1, mod1, True)
    return out.reshape(bsz, n, d)
```

```python
import functools

import jax
import jax.numpy as jnp
import numpy as np
from jax import lax
from jax.experimental import pallas as pl
from jax.experimental.pallas import tpu as pltpu

F32 = jnp.float32
BF16 = jnp.bfloat16

EPS = 1e-6
GRID_W = 64
CONV_CH = 512
CONV_WIDTH = 31
N_HEADS = 8
QK_NOPE = 64
QK_ROPE = 32
V_HEAD = 64
Q_LORA = 256
KV_LORA = 128
ROPE_BASE = 10000.0
N_EXPERTS = 64
TOP_K = 6
N_GROUPS = 8
TOPK_GROUPS = 4
ROUTED_SCALE = 2.5
ROUTE_BLOCK = 512

HEAD_LANES = 128
SEG_ROWS = 16
SORT_CHUNK = 256
VMEM_LIMIT = 56 * 1024 * 1024


def _cparams(*sem, **kw):
    return pltpu.CompilerParams(dimension_semantics=sem, vmem_limit_bytes=VMEM_LIMIT, **kw)


def _dot(a, b):
    return jnp.dot(a, b, preferred_element_type=F32)


def _dot_nt(a, b):
    return lax.dot_general(a, b, (((1,), (1,)), ((), ())), preferred_element_type=F32)


def _rms(x, w):
    return x * lax.rsqrt(jnp.mean(x * x, axis=-1, keepdims=True) + EPS) * w


def _silu(x):
    return x * jax.nn.sigmoid(x)


def _ada_kernel(c_ref, w_ref, b_ref, o_ref):
    c = c_ref[...]
    o_ref[0] = _dot(_silu(c), w_ref[0]) + b_ref[0]


def _ada(c16, ada_w, ada_b):
    depth, d, n6 = ada_w.shape
    tn = 1536
    return pl.pallas_call(
        _ada_kernel,
        grid=(depth, n6 // tn),
        in_specs=[
            pl.BlockSpec((16, d), lambda l, j: (0, 0)),
            pl.BlockSpec((1, d, tn), lambda l, j: (l, 0, j)),
            pl.BlockSpec((1, 1, tn), lambda l, j: (l, 0, j)),
        ],
        out_specs=pl.BlockSpec((1, 16, tn), lambda l, j: (l, 0, j)),
        out_shape=jax.ShapeDtypeStruct((depth, 16, n6), F32),
        compiler_params=_cparams("arbitrary", "arbitrary"),
        name="ada",
    )(c16, ada_w, ada_b.reshape(depth, 1, n6))


def _inproj0_kernel(x_ref, mod_ref, nw_ref, win_ref, qnw_ref, kvnw_ref, wq_ref, wkv_ref, e_ref,
                    ck_ref, cq_ref, *outs, latent, d):
    x = x_ref[...]
    mod = mod_ref[0]
    a = (_rms(x, nw_ref[...]) * (1.0 + mod[:, d:2 * d]) + mod[:, 0:d]).astype(BF16)
    u = _dot(a, win_ref[...])
    if latent:
        glu_ref, q_ref, k_ref, v_ref = outs
        glu_ref[...] = u[:, 0:CONV_CH] * jax.nn.sigmoid(u[:, CONV_CH:2 * CONV_CH])
        o = 2 * CONV_CH
        cqn = _rms(u[:, o:o + Q_LORA], qnw_ref[...]).astype(BF16)
        o += Q_LORA
        cosq = cq_ref[:, 0:HEAD_LANES]
        sinq = cq_ref[:, HEAD_LANES:2 * HEAD_LANES]
        nq = N_HEADS * HEAD_LANES
        qa = _dot(cqn, wq_ref[:, 0:nq])
        qb = _dot(cqn, wq_ref[:, nq:2 * nq])
        for h in range(N_HEADS):
            sl = slice(h * HEAD_LANES, (h + 1) * HEAD_LANES)
            q_ref[:, sl] = (qa[:, sl] * cosq + qb[:, sl] * sinq).astype(BF16)
    else:
        k_ref, v_ref = outs
        o = 0
    nk = N_HEADS * HEAD_LANES
    kvn = _rms(u[:, o:o + KV_LORA], kvnw_ref[...]).astype(BF16)
    o += KV_LORA
    prod = u[:, o:o + HEAD_LANES] * ck_ref[...]
    krot = (prod + pltpu.roll(prod, HEAD_LANES - QK_ROPE, 1)).astype(BF16)
    k = _dot(kvn, wkv_ref[:, 0:nk]) + _dot(krot, e_ref[...])
    k_ref[...] = k.astype(BF16)
    lane = lax.broadcasted_iota(jnp.int32, (1, nk), 1)
    ones = jnp.where(lane % HEAD_LANES == V_HEAD, 1.0, 0.0).astype(F32)
    v_ref[...] = (_dot(kvn, wkv_ref[:, nk:2 * nk]) + ones).astype(BF16)


def _inproj0(x2, mod3, mod_row_fn, nw, win, qnw, kvnw, wq, wkv, emat, ck, cq, *, latent, tm, seq):
    t, d = x2.shape
    nk = N_HEADS * HEAD_LANES
    tiles_per_seq = seq // tm
    full = lambda i: (0, 0)
    in_specs = [
        pl.BlockSpec((tm, d), lambda i: (i, 0)),
        pl.BlockSpec((1, 1, 6 * d), lambda i: (mod_row_fn(i), 0, 0)),
        pl.BlockSpec((1, d), full),
        pl.BlockSpec(win.shape, full),
        pl.BlockSpec((1, Q_LORA), full),
        pl.BlockSpec((1, KV_LORA), full),
        pl.BlockSpec(wq.shape, full),
        pl.BlockSpec(wkv.shape, full),
        pl.BlockSpec(emat.shape, full),
        pl.BlockSpec((tm, HEAD_LANES), lambda i: (i % tiles_per_seq, 0)),
        pl.BlockSpec((tm, 2 * HEAD_LANES), lambda i: (i % tiles_per_seq, 0)),
    ]
    row = lambda i: (i, 0)
    out_specs = [pl.BlockSpec((tm, nk), row), pl.BlockSpec((tm, nk), row)]
    out_shape = [jax.ShapeDtypeStruct((t, nk), BF16), jax.ShapeDtypeStruct((t, nk), BF16)]
    if latent:
        out_specs = [pl.BlockSpec((tm, CONV_CH), row), pl.BlockSpec((tm, nk), row)] + out_specs
        out_shape = [jax.ShapeDtypeStruct((t, CONV_CH), F32), jax.ShapeDtypeStruct((t, nk), BF16)] + out_shape
    return pl.pallas_call(
        functools.partial(_inproj0_kernel, latent=latent, d=d),
        grid=(t // tm,),
        in_specs=in_specs,
        out_specs=out_specs,
        out_shape=out_shape,
        compiler_params=_cparams("arbitrary"),
        name="inproj0_lat" if latent else "inproj0_ctx",
    )(x2, mod3, nw, win, qnw, kvnw, wq, wkv, emat, ck, cq)


CONV_HALO = 16
CONV_ROWS = 64


def _conv_kernel(main_ref, prev_ref, next_ref, w_ref, b_ref, lnw_ref, lnb_ref, o_ref, win_ref, sh_ref, *,
                 tn, tiles_per_seq):
    i = pl.program_id(0)
    first = (i % tiles_per_seq) == 0
    last = (i % tiles_per_seq) == tiles_per_seq - 1
    win_ref[0:CONV_HALO, :] = jnp.where(first, 0.0, prev_ref[...])
    win_ref[CONV_HALO:CONV_HALO + tn, :] = main_ref[...]
    win_ref[CONV_HALO + tn:, :] = jnp.where(last, 0.0, next_ref[...])
    shl = sh_ref.shape[1]
    for r in range(1, 8):
        sh_ref[r - 1] = win_ref[r:r + shl, :]
    off0 = CONV_HALO - CONV_WIDTH // 2
    for c in range(tn // CONV_ROWS):
        acc = None
        for k in range(CONV_WIDTH):
            a, r = divmod(off0 + k, 8)
            lo = c * CONV_ROWS + 8 * a
            src = win_ref[lo:lo + CONV_ROWS, :] if r == 0 else sh_ref[r - 1, lo:lo + CONV_ROWS, :]
            term = src * w_ref[k:k + 1, :]
            acc = term if acc is None else acc + term
        y = acc + b_ref[...]
        mu = jnp.mean(y, axis=-1, keepdims=True)
        yc = y - mu
        var = jnp.mean(yc * yc, axis=-1, keepdims=True)
        yn = yc * lax.rsqrt(var + EPS) * lnw_ref[...] + lnb_ref[...]
        o_ref[c * CONV_ROWS:(c + 1) * CONV_ROWS, :] = _silu(yn).astype(BF16)


def _conv(glu, w, b, lnw, lnb, *, tn, seq):
    t, ch = glu.shape
    tiles_per_seq = seq // tn
    hb = tn // CONV_HALO
    nhb = t // CONV_HALO
    full = lambda i: (0, 0)
    return pl.pallas_call(
        functools.partial(_conv_kernel, tn=tn, tiles_per_seq=tiles_per_seq),
        grid=(t // tn,),
        in_specs=[
            pl.BlockSpec((tn, ch), lambda i: (i, 0)),
            pl.BlockSpec((CONV_HALO, ch), lambda i: (jnp.maximum(i * hb - 1, 0), 0)),
            pl.BlockSpec((CONV_HALO, ch), lambda i: (jnp.minimum((i + 1) * hb, nhb - 1), 0)),
            pl.BlockSpec(w.shape, full),
            pl.BlockSpec((1, ch), full),
            pl.BlockSpec((1, ch), full),
            pl.BlockSpec((1, ch), full),
        ],
        out_specs=pl.BlockSpec((tn, ch), lambda i: (i, 0)),
        out_shape=jax.ShapeDtypeStruct((t, ch), BF16),
        scratch_shapes=[pltpu.VMEM((tn + 2 * CONV_HALO, ch), F32),
                        pltpu.VMEM((7, tn + 2 * CONV_HALO - 8, ch), F32)],
        compiler_params=_cparams("arbitrary"),
        name="conv",
    )(glu, glu, glu, w, b, lnw, lnb)


ATT_KC = 512


ATT_HEADS = 4


def _attn_kernel(q_ref, kc_ref, kl_ref, vc_ref, vl_ref, o_ref, s_ref, *, nctx, nlat):
    nchunk = nlat // ATT_KC

    def keys(c):
        return (slice(0, nctx), slice(0, nctx)) if c < 0 else \
            (slice(c * ATT_KC, (c + 1) * ATT_KC), slice(nctx + c * ATT_KC, nctx + (c + 1) * ATT_KC))

    def score_chunk(h, c, m):
        sl = slice(h * HEAD_LANES, (h + 1) * HEAD_LANES)
        rows, cols = keys(c)
        k = kc_ref[0, rows, sl] if c < 0 else kl_ref[0, rows, sl]
        s = _dot_nt(q_ref[0, :, sl], k)
        s_ref[h % 2, :, cols] = s
        for i in range(s.shape[1] // HEAD_LANES):
            part = s[:, i * HEAD_LANES:(i + 1) * HEAD_LANES]
            m = part if m is None else jnp.maximum(m, part)
        return m

    def value_chunk(h, c, m, acc):
        sl = slice(h * HEAD_LANES, (h + 1) * HEAD_LANES)
        rows, cols = keys(c)
        v = vc_ref[0, rows, sl] if c < 0 else vl_ref[0, rows, sl]
        p = jnp.exp2((s_ref[h % 2, :, cols] - m).astype(BF16))
        pv = _dot(p, v)
        return pv if acc is None else acc + pv

    mp = None
    for c in range(-1, nchunk):
        mp = score_chunk(0, c, mp)
    outs = []
    for h in range(ATT_HEADS):
        m = jnp.max(mp, axis=-1, keepdims=True)
        acc, mp = None, None
        for c in range(-1, nchunk):
            acc = value_chunk(h, c, m, acc)
            if h + 1 < ATT_HEADS:
                mp = score_chunk(h + 1, c, mp)
        outs.append(acc[:, 0:V_HEAD] / acc[:, V_HEAD:V_HEAD + 1])
    o_ref[0] = jnp.concatenate(outs, axis=-1).astype(BF16)


def _attn(q, kc, kl, vc, vl, *, tq):
    b, n, _ = q.shape
    nctx = kc.shape[1]
    hp = N_HEADS // ATT_HEADS
    w = ATT_HEADS * HEAD_LANES
    return pl.pallas_call(
        functools.partial(_attn_kernel, nctx=nctx, nlat=n),
        grid=(b, hp, n // tq),
        in_specs=[
            pl.BlockSpec((1, tq, w), lambda bi, j, i: (bi, i, j)),
            pl.BlockSpec((1, nctx, w), lambda bi, j, i: (bi, 0, j)),
            pl.BlockSpec((1, n, w), lambda bi, j, i: (bi, 0, j)),
            pl.BlockSpec((1, nctx, w), lambda bi, j, i: (bi, 0, j)),
            pl.BlockSpec((1, n, w), lambda bi, j, i: (bi, 0, j)),
        ],
        out_specs=pl.BlockSpec((1, tq, ATT_HEADS * V_HEAD), lambda bi, j, i: (bi, i, j)),
        out_shape=jax.ShapeDtypeStruct((b, n, N_HEADS * V_HEAD), BF16),
        scratch_shapes=[pltpu.VMEM((2, tq, nctx + n), F32)],
        compiler_params=_cparams("arbitrary", "arbitrary", "arbitrary"),
        name="attn",
    )(q, kc, kl, vc, vl)


def _route(lt, rb_ref, utri_ref, ltri_ref, ls_ref, wn_ref, seg_ref, tm):
    ninf = -jnp.inf
    gsz = N_EXPERTS // N_GROUPS
    s = jax.nn.sigmoid(lt)
    sel = s + rb_ref[...]
    sub = lax.broadcasted_iota(jnp.int32, (gsz, tm), 0).astype(F32)
    gs = []
    for g in range(N_GROUPS):
        xg = sel[g * gsz:(g + 1) * gsz]
        m1 = jnp.max(xg, axis=0, keepdims=True)
        i1 = jnp.min(jnp.where(xg == m1, sub, float(gsz)), axis=0, keepdims=True)
        m2 = jnp.max(jnp.where(sub == i1, ninf, xg), axis=0, keepdims=True)
        gs.append(m1 + m2)
    rows = []
    for g in range(N_GROUPS):
        beaten = jnp.zeros((1, tm), F32)
        for o in range(N_GROUPS):
            if o != g:
                beats = (gs[o] >= gs[g]) if o < g else (gs[o] > gs[g])
                beaten = beaten + jnp.where(beats, 1.0, 0.0)
        rows.append(jnp.where(beaten < float(TOPK_GROUPS), sel[g * gsz:(g + 1) * gsz], ninf))
    cur = jnp.concatenate(rows, axis=0)
    ei = lax.broadcasted_iota(jnp.int32, (N_EXPERTS, tm), 0).astype(F32)
    hits, ws = [], []
    for k in range(TOP_K):
        m = jnp.max(cur, axis=0, keepdims=True)
        idx = jnp.min(jnp.where(cur == m, ei, float(N_EXPERTS)), axis=0, keepdims=True)
        hit = ei == idx
        ws.append(jnp.sum(jnp.where(hit, s, 0.0), axis=0, keepdims=True))
        cur = jnp.where(hit, ninf, cur)
        hits.append(jnp.where(hit, 1.0, 0.0))
    wsum = ws[0]
    for k in range(1, TOP_K):
        wsum = wsum + ws[k]
    pre = _dot(jnp.concatenate(hits, axis=0).astype(BF16), utri_ref[...])
    tots = [jnp.sum(h, axis=1, keepdims=True) for h in hits]
    cnt = tots[0]
    for k in range(1, TOP_K):
        cnt = cnt + tots[k]
    seg = jnp.floor((cnt + (SEG_ROWS - 1.0)) * (1.0 / SEG_ROWS))
    segb = jnp.broadcast_to(seg, (N_EXPERTS, HEAD_LANES))
    seg_ref[0] = segb
    lstart = _dot(ltri_ref[...], segb.astype(BF16))[:, 0:1] * float(SEG_ROWS)
    basek = lstart
    for k in range(TOP_K):
        rank = jnp.sum(hits[k] * (pre[k * N_EXPERTS:(k + 1) * N_EXPERTS] + basek), axis=0, keepdims=True)
        ls_ref[k:k + 1, :] = rank.astype(jnp.int32)
        wn_ref[k:k + 1, :] = ws[k] / wsum * ROUTED_SCALE
        basek = basek + tots[k]
    ls_ref[TOP_K:8, :] = jnp.full((8 - TOP_K, tm), -1, jnp.int32)
    wn_ref[TOP_K:8, :] = jnp.zeros((8 - TOP_K, tm), F32)


def _tail(y, x, mod, d, tm, nfw_ref, swg_ref, swu_ref, swd_ref, rwh_ref, rwl_ref, rb_ref, utri_ref, ltri_ref,
          fb_ref, base_ref, ls_ref, wn_ref, seg_ref):
    g1 = mod[:, 2 * d:3 * d]
    sh2 = mod[:, 3 * d:4 * d]
    sc2 = mod[:, 4 * d:5 * d]
    g2 = mod[:, 5 * d:6 * d]
    h1 = x + g1 * y
    f = _rms(h1, nfw_ref[...]) * (1.0 + sc2) + sh2
    fb = f.astype(BF16)
    fb_ref[...] = fb
    act = (_silu(_dot(fb, swg_ref[...])) * _dot(fb, swu_ref[...])).astype(BF16)
    base_ref[...] = h1 + g2 * _dot(act, swd_ref[...])
    flo = (f - fb.astype(F32)).astype(BF16)
    logits = _dot(fb, rwh_ref[...]) + (_dot(flo, rwh_ref[...]) + _dot(fb, rwl_ref[...]))
    _route(logits.T[0:N_EXPERTS], rb_ref, utri_ref, ltri_ref, ls_ref, wn_ref, seg_ref, tm)


def _mix0_kernel(oa_ref, ob_ref, x_ref, mod_ref, woa_ref, wob_ref, *rest, d, tm):
    y = _dot(oa_ref[...], woa_ref[...]) + _dot(ob_ref[...], wob_ref[...])
    _tail(y, x_ref[...], mod_ref[0], d, tm, *rest)


def _mix1_kernel(b_ref, z_ref, zp_ref, zn_ref, x_ref, mod_ref, cw_ref, wo_ref, *rest, d, tm, tiles_per_seq):
    i = pl.program_id(0)
    first = (i % tiles_per_seq) == 0
    last = (i % tiles_per_seq) == tiles_per_seq - 1
    z = z_ref[...].astype(F32)
    rowi = lax.broadcasted_iota(jnp.int32, (tm, 1), 0)
    zprev_row = jnp.where(first, 0.0, zp_ref[7:8, :].astype(F32))
    znext_row = jnp.where(last, 0.0, zn_ref[0:1, :].astype(F32))
    zm = jnp.where(rowi == 0, zprev_row, pltpu.roll(z, 1, 0))
    zp = jnp.where(rowi == tm - 1, znext_row, pltpu.roll(z, tm - 1, 0))
    conv = zm * cw_ref[0:1, :] + z * cw_ref[1:2, :] + zp * cw_ref[2:3, :]
    v = (b_ref[...].astype(F32) * conv).astype(BF16)
    _tail(_dot(v, wo_ref[...]), x_ref[...], mod_ref[0], d, tm, *rest)


def _tail_specs(t, d, tm, tail_w):
    full = lambda i: (0, 0)
    row = lambda i: (i, 0)
    col = lambda i: (0, i)
    in_specs = [pl.BlockSpec(w.shape, full) for w in tail_w]
    out_specs = [pl.BlockSpec((tm, d), row), pl.BlockSpec((tm, d), row), pl.BlockSpec((8, tm), col),
                 pl.BlockSpec((8, tm), col), pl.BlockSpec((1, N_EXPERTS, HEAD_LANES), lambda i: (i, 0, 0))]
    out_shape = [jax.ShapeDtypeStruct((t, d), BF16), jax.ShapeDtypeStruct((t, d), F32),
                 jax.ShapeDtypeStruct((8, t), jnp.int32), jax.ShapeDtypeStruct((8, t), F32),
                 jax.ShapeDtypeStruct((t // tm, N_EXPERTS, HEAD_LANES), F32)]
    return in_specs, out_specs, out_shape


def _mix0(oa, ob, x2, mod3, woa, wob, tail_w, *, tm, seq):
    t, d = x2.shape
    full = lambda i: (0, 0)
    row = lambda i: (i, 0)
    tin, out_specs, out_shape = _tail_specs(t, d, tm, tail_w)
    return pl.pallas_call(
        functools.partial(_mix0_kernel, d=d, tm=tm),
        grid=(t // tm,),
        in_specs=[pl.BlockSpec((tm, oa.shape[1]), row), pl.BlockSpec((tm, ob.shape[1]), row),
                  pl.BlockSpec((tm, d), row),
                  pl.BlockSpec((1, 1, 6 * d), lambda i: ((i * tm) // seq, 0, 0)),
                  pl.BlockSpec(woa.shape, full), pl.BlockSpec(wob.shape, full)] + tin,
        out_specs=out_specs,
        out_shape=out_shape,
        compiler_params=_cparams("arbitrary"),
        name="mix0_tail",
    )(oa, ob, x2, mod3, woa, wob, *tail_w)


def _mix1(bq, z, x2, mod3, cw, wo, tail_w, *, tm, seq):
    t, d = x2.shape
    full = lambda i: (0, 0)
    row = lambda i: (i, 0)
    hb = tm // 8
    nhb = t // 8
    tin, out_specs, out_shape = _tail_specs(t, d, tm, tail_w)
    return pl.pallas_call(
        functools.partial(_mix1_kernel, d=d, tm=tm, tiles_per_seq=seq // tm),
        grid=(t // tm,),
        in_specs=[pl.BlockSpec((tm, d), row), pl.BlockSpec((tm, d), row),
                  pl.BlockSpec((8, d), lambda i: (jnp.maximum(i * hb - 1, 0), 0)),
                  pl.BlockSpec((8, d), lambda i: (jnp.minimum((i + 1) * hb, nhb - 1), 0)),
                  pl.BlockSpec((tm, d), row),
                  pl.BlockSpec((1, 1, 6 * d), lambda i: ((i * tm) // seq, 0, 0)),
                  pl.BlockSpec(cw.shape, full), pl.BlockSpec(wo.shape, full)] + tin,
        out_specs=out_specs,
        out_shape=out_shape,
        compiler_params=_cparams("arbitrary"),
        name="mix1_tail",
    )(bq, z, z, z, x2, mod3, cw, wo, *tail_w)


def _inproj1_kernel(x_ref, mod_ref, nw_ref, win_ref, b_ref, z_ref, *, d):
    mod = mod_ref[0]
    a = (_rms(x_ref[...], nw_ref[...]) * (1.0 + mod[:, d:2 * d]) + mod[:, 0:d]).astype(BF16)
    u = _dot(a, win_ref[...])
    b_ref[...] = u[:, 0:d].astype(BF16)
    z_ref[...] = (u[:, d:2 * d] * u[:, 2 * d:3 * d]).astype(F32)


def _inproj1(x2, mod3, nw, win, *, tm, seq):
    t, d = x2.shape
    full = lambda i: (0, 0)
    row = lambda i: (i, 0)
    return pl.pallas_call(
        functools.partial(_inproj1_kernel, d=d),
        grid=(t // tm,),
        in_specs=[pl.BlockSpec((tm, d), row),
                  pl.BlockSpec((1, 1, 6 * d), lambda i: ((i * tm) // seq, 0, 0)),
                  pl.BlockSpec((1, d), full), pl.BlockSpec(win.shape, full)],
        out_specs=[pl.BlockSpec((tm, d), row), pl.BlockSpec((tm, d), row)],
        out_shape=[jax.ShapeDtypeStruct((t, d), BF16), jax.ShapeDtypeStruct((t, d), F32)],
        compiler_params=_cparams("arbitrary"),
        name="inproj1",
    )(x2, mod3, nw, win)


def _seg_copies(dstg_ref, ntot_ref, j, local, slot, glob, sem, to_global):
    per_tile = local.shape[1] // SEG_ROWS

    def gbody(q, carry):
        lrow = pl.multiple_of(q * SEG_ROWS, SEG_ROWS)
        grow = pl.multiple_of(dstg_ref[j * per_tile + q] * SEG_ROWS, SEG_ROWS)
        lref = local.at[slot, pl.ds(lrow, SEG_ROWS)]
        gref = glob.at[pl.ds(grow, SEG_ROWS)]
        if to_global:
            pltpu.make_async_copy(lref, gref, sem.at[slot]).start()
        else:
            pltpu.make_async_copy(gref, lref, sem.at[slot]).start()
        return carry
    lax.fori_loop(0, ntot_ref[j], gbody, 0)


def _seg_wait(n, local, slot, glob, sem):
    p = local.shape[1] // SEG_ROWS
    while p >= 1:
        @pl.when((n & p) != 0)
        def _(p=p):
            rows = p * SEG_ROWS
            pltpu.make_async_copy(glob.at[pl.ds(0, rows)], local.at[slot, pl.ds(0, rows)], sem.at[slot]).wait()
        p //= 2


def _sorted_onehot(ls_f, rc, vals):
    tm = ls_f.shape[1]
    riota = lax.broadcasted_iota(jnp.int32, (SORT_CHUNK, tm), 0).astype(F32).astype(BF16)
    rel = jnp.clip(ls_f - float(rc * SORT_CHUNK), -1.0, float(SORT_CHUNK)).astype(BF16)
    p = jnp.zeros((SORT_CHUNK, tm), BF16)
    for k in range(TOP_K):
        val = jnp.ones((1, tm), BF16) if vals is None else vals[k:k + 1, :]
        p = jnp.where(riota == rel[k:k + 1, :], val, p)
    return p


def _dispatch_kernel(dstg_ref, ntot_ref, gap_ref, ls_ref, f_ref, xs_hbm, loc, zbuf, sem, zsem, *,
                     tm, lrows):
    j = pl.program_id(0)
    nj = pl.num_programs(0)
    slot = j % 2

    @pl.when(j >= 2)
    def _():
        _seg_wait(ntot_ref[j - 2], loc, slot, xs_hbm, sem)

    nrows = ntot_ref[j] * SEG_ROWS
    ls_f = ls_ref[...].astype(F32)
    def sort_chunk(rc):
        p = _sorted_onehot(ls_f, rc, None)
        loc[slot, rc * SORT_CHUNK:(rc + 1) * SORT_CHUNK, :] = _dot(p, f_ref[...]).astype(BF16)

    nfix = TOP_K * tm // SORT_CHUNK
    for rc in range(nfix):
        sort_chunk(rc)
    for rc in range(nfix, lrows // SORT_CHUNK):
        pl.when(rc * SORT_CHUNK < nrows)(functools.partial(sort_chunk, rc))

    _seg_copies(dstg_ref, ntot_ref, j, loc, slot, xs_hbm, sem, True)

    @pl.when(j == nj - 1)
    def _():
        zbuf[...] = jnp.zeros(zbuf.shape, BF16)
        zgran = zbuf.at[pl.ds(0, SEG_ROWS)]

        def zbody(e, carry):
            def gbody(g, c2):
                grow = pl.multiple_of((gap_ref[e] + g) * SEG_ROWS, SEG_ROWS)
                pltpu.make_async_copy(zgran, xs_hbm.at[pl.ds(grow, SEG_ROWS)], zsem.at[0]).start()
                return c2
            lax.fori_loop(0, gap_ref[N_EXPERTS + e], gbody, 0)
            return carry
        lax.fori_loop(0, N_EXPERTS, zbody, 0)

        def tbody(b, carry):
            brow = pl.multiple_of(b * ROUTE_BLOCK, ROUTE_BLOCK)
            pltpu.make_async_copy(zbuf, xs_hbm.at[pl.ds(brow, ROUTE_BLOCK)], zsem.at[1]).start()
            return carry
        nblocks = xs_hbm.shape[0] // ROUTE_BLOCK
        lax.fori_loop(gap_ref[2 * N_EXPERTS + 1], nblocks, tbody, 0)

        def zwait(i, carry):
            pltpu.make_async_copy(zgran, xs_hbm.at[pl.ds(0, SEG_ROWS)], zsem.at[0]).wait()
            return carry
        lax.fori_loop(0, gap_ref[2 * N_EXPERTS], zwait, 0)

        def twait(b, carry):
            pltpu.make_async_copy(zbuf, xs_hbm.at[pl.ds(0, ROUTE_BLOCK)], zsem.at[1]).wait()
            return carry
        lax.fori_loop(gap_ref[2 * N_EXPERTS + 1], nblocks, twait, 0)
        _seg_wait(ntot_ref[j], loc, slot, xs_hbm, sem)

        @pl.when(j >= 1)
        def _():
            _seg_wait(ntot_ref[j - 1], loc, 1 - slot, xs_hbm, sem)


def _dispatch(plan, ls, fb, *, tm, lrows, prows):
    t, d = fb.shape
    grid_spec = pltpu.PrefetchScalarGridSpec(
        num_scalar_prefetch=3,
        grid=(t // tm,),
        in_specs=[pl.BlockSpec((8, tm), lambda j, *_: (0, j)), pl.BlockSpec((tm, d), lambda j, *_: (j, 0))],
        out_specs=pl.BlockSpec(memory_space=pl.ANY),
        scratch_shapes=[pltpu.VMEM((2, lrows, d), BF16), pltpu.VMEM((ROUTE_BLOCK, d), BF16),
                        pltpu.SemaphoreType.DMA((2,)), pltpu.SemaphoreType.DMA((2,))],
    )
    return pl.pallas_call(
        functools.partial(_dispatch_kernel, tm=tm, lrows=lrows),
        grid_spec=grid_spec,
        out_shape=jax.ShapeDtypeStruct((prows, d), BF16),
        compiler_params=_cparams("arbitrary"),
        name="dispatch",
    )(plan["dstg"], plan["ntot"], plan["gap"], ls, fb)


EXP_RING = 4


def _experts_kernel(bnd_ref, wg_ref, wu_ref, wd_ref, xs_hbm, ys_hbm, wgb, wub, wdb, xbuf, ybuf, xsem, ysem):
    e = pl.program_id(0)
    b0 = bnd_ref[e]
    b1 = bnd_ref[e + 1]
    nv = bnd_ref[N_EXPERTS]
    nblocks = xs_hbm.shape[0] // ROUTE_BLOCK

    def xcopy(b, slot):
        row = pl.multiple_of(b * ROUTE_BLOCK, ROUTE_BLOCK)
        return pltpu.make_async_copy(xs_hbm.at[pl.ds(row, ROUTE_BLOCK)], xbuf.at[slot], xsem.at[slot])

    def ycopy(b, slot):
        row = pl.multiple_of(b * ROUTE_BLOCK, ROUTE_BLOCK)
        return pltpu.make_async_copy(ybuf.at[slot], ys_hbm.at[pl.ds(row, ROUTE_BLOCK)], ysem.at[slot])

    @pl.when(e == 0)
    def _():
        for i in range(EXP_RING - 1):
            @pl.when(i < nv)
            def _(i=i):
                xcopy(i, i).start()

    @pl.when(b1 > b0)
    def _():
        wgb[...] = wg_ref[0, 0].astype(BF16)
        wub[...] = wu_ref[0, 0].astype(BF16)
        wdb[...] = wd_ref[0, 0].astype(BF16)

    def body(b, carry):
        slot = b % EXP_RING
        ahead = b + (EXP_RING - 1)

        @pl.when(ahead < nv)
        def _():
            xcopy(ahead, ahead % EXP_RING).start()

        xcopy(b, slot).wait()

        @pl.when(b >= EXP_RING)
        def _():
            ycopy(b - EXP_RING, slot).wait()

        x = xbuf[slot]
        act = (_silu(_dot(x, wgb[...])) * _dot(x, wub[...])).astype(BF16)
        ybuf[slot] = _dot(act, wdb[...]).astype(BF16)
        ycopy(b, slot).start()
        return carry
    lax.fori_loop(b0, b1, body, 0)

    @pl.when(e == N_EXPERTS - 1)
    def _():
        for i in range(1, EXP_RING + 1):
            @pl.when(nv >= i)
            def _(i=i):
                ycopy(nv - i, (nv - i) % EXP_RING).wait()

        ybuf[0] = jnp.zeros(ybuf.shape[1:], BF16)

        def tbody(b, carry):
            ycopy(b, 0).start()
            return carry
        lax.fori_loop(nv, nblocks, tbody, 0)

        def twait(b, carry):
            ycopy(b, 0).wait()
            return carry
        lax.fori_loop(nv, nblocks, twait, 0)


def _experts(bounds, xs, wg, wu, wd, layer):
    prows, d = xs.shape
    de = wg.shape[3]
    wsel = lambda e, bnd: (layer, e, 0, 0)
    grid_spec = pltpu.PrefetchScalarGridSpec(
        num_scalar_prefetch=1,
        grid=(N_EXPERTS,),
        in_specs=[pl.BlockSpec((1, 1, d, de), wsel), pl.BlockSpec((1, 1, d, de), wsel),
                  pl.BlockSpec((1, 1, de, d), wsel), pl.BlockSpec(memory_space=pl.ANY)],
        out_specs=pl.BlockSpec(memory_space=pl.ANY),
        scratch_shapes=[pltpu.VMEM((d, de), BF16), pltpu.VMEM((d, de), BF16), pltpu.VMEM((de, d), BF16),
                        pltpu.VMEM((EXP_RING, ROUTE_BLOCK, d), BF16), pltpu.VMEM((EXP_RING, ROUTE_BLOCK, d), BF16),
                        pltpu.SemaphoreType.DMA((EXP_RING,)), pltpu.SemaphoreType.DMA((EXP_RING,))],
    )
    return pl.pallas_call(
        _experts_kernel,
        grid_spec=grid_spec,
        out_shape=jax.ShapeDtypeStruct((prows, d), BF16),
        compiler_params=_cparams("arbitrary"),
        name="experts",
    )(bounds, wg, wu, wd, xs)


COMB_GROUP = 4


def _combine_kernel(dstg_ref, ntot_ref, ls_ref, wn_ref, base_ref, mod_ref, fnw_ref, ys_hbm,
                    o_ref, loc, pw, acc, sem, *, tm, lrows, d, final):
    j = pl.program_id(0)
    nj = pl.num_programs(0)
    slot = j % 2

    @pl.when(j == 0)
    def _():
        loc[...] = jnp.zeros(loc.shape, BF16)
        _seg_copies(dstg_ref, ntot_ref, 0, loc, 0, ys_hbm, sem, False)

    @pl.when(j + 1 < nj)
    def _():
        _seg_copies(dstg_ref, ntot_ref, j + 1, loc, 1 - slot, ys_hbm, sem, False)

    ls_f = ls_ref[...].astype(F32)
    wn_b = wn_ref[...].astype(BF16)
    nfix = TOP_K * tm // SORT_CHUNK
    _seg_wait(ntot_ref[j], loc, slot, ys_hbm, sem)
    tn_dims = (((0,), (0,)), ((), ()))
    grp = COMB_GROUP * SORT_CHUNK
    total = None
    for g in range(nfix // COMB_GROUP):
        for rc in range(g * COMB_GROUP, (g + 1) * COMB_GROUP):
            pw[rc * SORT_CHUNK:(rc + 1) * SORT_CHUNK, :] = _sorted_onehot(ls_f, rc, wn_b)
        part = lax.dot_general(pw[g * grp:(g + 1) * grp, :], loc[slot, g * grp:(g + 1) * grp, :], tn_dims,
                               preferred_element_type=F32)
        total = part if total is None else total + part
    acc[...] = total

    nrows = ntot_ref[j] * SEG_ROWS
    for rc in range(nfix, lrows // SORT_CHUNK):
        @pl.when(rc * SORT_CHUNK < nrows)
        def _():
            ysc = loc[slot, rc * SORT_CHUNK:(rc + 1) * SORT_CHUNK, :]
            acc[...] += lax.dot_general(_sorted_onehot(ls_f, rc, wn_b), ysc, tn_dims, preferred_element_type=F32)

    h = base_ref[...] + mod_ref[0][:, 5 * d:6 * d] * acc[...]
    if final:
        h = _rms(h, fnw_ref[...])
    o_ref[...] = h


def _combine(plan, ls, wn, ys, base, mod3, fnw, *, tm, lrows, seq, final):
    t, d = base.shape
    full = lambda j, *_: (0, 0)
    row = lambda j, *_: (j, 0)
    col = lambda j, *_: (0, j)
    grid_spec = pltpu.PrefetchScalarGridSpec(
        num_scalar_prefetch=2,
        grid=(t // tm,),
        in_specs=[pl.BlockSpec((8, tm), col), pl.BlockSpec((8, tm), col), pl.BlockSpec((tm, d), row),
                  pl.BlockSpec((1, 1, 6 * d), lambda j, *_: ((j * tm) // seq, 0, 0)),
                  pl.BlockSpec((1, d), full), pl.BlockSpec(memory_space=pl.ANY)],
        out_specs=pl.BlockSpec((tm, d), row),
        scratch_shapes=[pltpu.VMEM((2, lrows, d), BF16), pltpu.VMEM((TOP_K * tm, tm), BF16),
                        pltpu.VMEM((tm, d), F32), pltpu.SemaphoreType.DMA((2,))],
    )
    return pl.pallas_call(
        functools.partial(_combine_kernel, tm=tm, lrows=lrows, d=d, final=final),
        grid_spec=grid_spec,
        out_shape=jax.ShapeDtypeStruct((t, d), F32),
        compiler_params=_cparams("arbitrary"),
        name="combine_final" if final else "combine",
    )(plan["dstg"], plan["ntot"], ls, wn, base, mod3, fnw, ys)


def _dispatch_plan(seg_out, per_tile):
    seg = seg_out[:, :, 0].astype(jnp.int32)
    gpb = ROUTE_BLOCK // SEG_ROWS
    lend = jnp.cumsum(seg, axis=1)
    lst = lend - seg
    tile_off = jnp.cumsum(seg, axis=0) - seg
    tot = jnp.sum(seg, axis=0)
    region_blk = (tot + gpb - 1) // gpb
    gend_blk = jnp.cumsum(region_blk)
    gstart = (gend_blk - region_blk) * gpb
    goff = gstart[None, :] + tile_off
    gap_cnt = region_blk * gpb - tot
    gap = jnp.concatenate([gstart + tot, gap_cnt, jnp.sum(gap_cnt)[None], gend_blk[-1:]]).astype(jnp.int32)
    bounds = jnp.concatenate([jnp.zeros((1,), jnp.int32), gend_blk.astype(jnp.int32)])
    q = jnp.arange(per_tile, dtype=jnp.int32)[None, :, None]
    owner = (q >= lst[:, None, :]) & (q < lend[:, None, :])
    dstg = jnp.sum(jnp.where(owner, (goff - lst)[:, None, :], 0), axis=-1) + q[:, :, 0]
    plan = {"dstg": dstg.reshape(-1).astype(jnp.int32), "ntot": lend[:, -1].astype(jnp.int32), "gap": gap}
    return plan, bounds


def _rope_tables(n):
    rows = n // GRID_W
    row = jnp.repeat(jnp.arange(rows), GRID_W).astype(F32)
    col = jnp.tile(jnp.arange(GRID_W), rows).astype(F32)
    half = QK_ROPE // 2
    inv = 1.0 / (ROPE_BASE ** (jnp.arange(0, half, 2, dtype=F32) / half))
    ang = jnp.stack([row[:, None] * inv, col[:, None] * inv], axis=1)
    cos, sin = jnp.cos(ang), jnp.sin(ang)
    c32 = jnp.broadcast_to(cos[:, :, None, :], (n, 2, 2, half // 2)).reshape(n, QK_ROPE)
    s32 = jnp.broadcast_to(sin[:, :, None, :], (n, 2, 2, half // 2)).reshape(n, QK_ROPE)
    return c32, s32


def _swap_signed(w):
    wr = w.reshape(w.shape[:-1] + (2, 2, QK_ROPE // 4))
    return jnp.stack([-wr[..., 1, :], wr[..., 0, :]], axis=-2).reshape(w.shape)


def _head_group(parts, lead):
    width = sum(p.shape[-1] for p in parts)
    pad = jnp.zeros(lead + (N_HEADS, HEAD_LANES - width), F32)
    return jnp.concatenate(list(parts) + [pad], axis=-1).reshape(lead + (N_HEADS * HEAD_LANES,))


def kernel(x, c, ctx, c_ctx, ada_w, ada_b, norm_mix_w, norm_ffn_w, ev_w_in, ev_conv_w, ev_conv_b, ev_ln_w,
           ev_ln_b, ev_q_norm_w, ev_kv_norm_w, ev_w_uq, ev_w_ukv, ev_w_out, od_w_in, od_conv_w, od_w_out,
           router_w, router_bias, exp_w_gate, exp_w_up, exp_w_down, sh_w_gate, sh_w_up, sh_w_down,
           final_norm_w):
    bsz, n, d = x.shape
    nctx = ctx.shape[1]
    t = bsz * n
    tm = 512
    x2 = x.reshape(t, d)

    c16 = jnp.zeros((16, d), F32).at[0:bsz].set(c).at[bsz].set(c_ctx)
    mod = _ada(c16, ada_w, ada_b)
    mod0 = mod[0].reshape(16, 1, 6 * d)
    mod1 = mod[1].reshape(16, 1, 6 * d)

    w_in = ev_w_in[0]
    o_kr = 2 * CONV_CH + Q_LORA + KV_LORA
    w_kr = w_in[:, o_kr:o_kr + QK_ROPE]
    win_ext = jnp.concatenate(
        [w_in[:, :o_kr], w_kr, _swap_signed(w_kr), jnp.zeros((d, HEAD_LANES - 2 * QK_ROPE), F32)], axis=1).astype(BF16)
    wuq = ev_w_uq[0]
    wq_rope = wuq[..., QK_NOPE:]
    zq = jnp.zeros((Q_LORA, N_HEADS, QK_NOPE), F32)
    wq_ext = jnp.concatenate([_head_group([wuq], (Q_LORA,)),
                              _head_group([zq, _swap_signed(wq_rope)], (Q_LORA,))], axis=1).astype(BF16)
    wukv = ev_w_ukv[0]
    wkv_ext = jnp.concatenate([_head_group([wukv[..., :QK_NOPE]], (KV_LORA,)),
                               _head_group([wukv[..., QK_NOPE:]], (KV_LORA,))], axis=1).astype(BF16)
    eye = jnp.eye(QK_ROPE, dtype=F32)
    e_small = _head_group([jnp.zeros((QK_ROPE, N_HEADS, QK_NOPE), F32),
                           jnp.broadcast_to(eye[:, None, :], (QK_ROPE, N_HEADS, QK_ROPE))], (QK_ROPE,))
    emat = jnp.concatenate([e_small, jnp.zeros((HEAD_LANES - QK_ROPE, N_HEADS * HEAD_LANES), F32)], axis=0).astype(BF16)

    c32, s32 = _rope_tables(n)
    scale = (QK_NOPE + QK_ROPE) ** -0.5 * float(np.log2(np.e))
    zpad = jnp.zeros((n, HEAD_LANES - 2 * QK_ROPE), F32)
    ck_lat = jnp.concatenate([c32, s32, zpad], axis=1)
    zrope = jnp.zeros((n, HEAD_LANES - QK_NOPE - QK_ROPE), F32)
    cq_lat = jnp.concatenate([jnp.full((n, QK_NOPE), scale, F32), c32 * scale, zrope,
                              jnp.zeros((n, QK_NOPE), F32), s32 * scale, zrope], axis=1)
    ck_ctx = jnp.concatenate([jnp.ones((nctx, QK_ROPE), F32), jnp.zeros((nctx, HEAD_LANES - QK_ROPE), F32)], axis=1)
    cq_ctx = jnp.zeros((nctx, 2 * HEAD_LANES), F32)

    nw0 = norm_mix_w[0].reshape(1, d)
    qnw = ev_q_norm_w[0].reshape(1, Q_LORA)
    kvnw = ev_kv_norm_w[0].reshape(1, KV_LORA)
    glu, q, k_l, v_l = _inproj0(x2, mod0, lambda i: (i * tm) // n, nw0, win_ext, qnw, kvnw, wq_ext, wkv_ext,
                                emat, ck_lat, cq_lat, latent=True, tm=tm, seq=n)
    tmc = min(tm, nctx)
    k_c, v_c = _inproj0(ctx.reshape(bsz * nctx, d), mod0, lambda i: bsz, nw0,
                        win_ext[:, 2 * CONV_CH + Q_LORA:], qnw, kvnw, wq_ext, wkv_ext,
                        emat, ck_ctx, cq_ctx, latent=False, tm=tmc, seq=nctx)

    o_a = _conv(glu, ev_conv_w[0].reshape(CONV_WIDTH, CONV_CH), ev_conv_b[0].reshape(1, CONV_CH),
                ev_ln_w[0].reshape(1, CONV_CH), ev_ln_b[0].reshape(1, CONV_CH), tn=tm, seq=n)
    nk = N_HEADS * HEAD_LANES
    o_b = _attn(q.reshape(bsz, n, nk), k_c.reshape(bsz, nctx, nk), k_l.reshape(bsz, n, nk),
                v_c.reshape(bsz, nctx, nk), v_l.reshape(bsz, n, nk), tq=512)
    o_b = o_b.reshape(t, N_HEADS * V_HEAD)

    utri = (jnp.arange(tm)[:, None] < jnp.arange(tm)[None, :]).astype(BF16)
    ltri = (jnp.arange(N_EXPERTS)[None, :] < jnp.arange(N_EXPERTS)[:, None]).astype(BF16)

    def tail_weights(i):
        rw = jnp.concatenate([router_w[i], jnp.zeros((d, HEAD_LANES - N_EXPERTS), F32)], axis=1)
        rwh = rw.astype(BF16)
        rwl = (rw - rwh.astype(F32)).astype(BF16)
        return (norm_ffn_w[i].reshape(1, d), sh_w_gate[i].astype(BF16), sh_w_up[i].astype(BF16),
                sh_w_down[i].astype(BF16), rwh, rwl, router_bias[i].astype(F32).reshape(N_EXPERTS, 1), utri, ltri)

    ntiles = t // tm
    max_pad = SEG_ROWS - 1
    lrows = -(-(TOP_K * tm + N_EXPERTS * max_pad) // SORT_CHUNK) * SORT_CHUNK
    nblocks = -(-(TOP_K * t + ntiles * N_EXPERTS * max_pad + N_EXPERTS * (ROUTE_BLOCK - SEG_ROWS)) // ROUTE_BLOCK)
    fnw = final_norm_w.reshape(1, d)

    def moe(i, fb, base, ls, wn, seg_out, modi, final):
        plan, bounds = _dispatch_plan(seg_out, lrows // SEG_ROWS)
        xs = _dispatch(plan, ls, fb, tm=tm, lrows=lrows, prows=nblocks * ROUTE_BLOCK)
        ys = _experts(bounds, xs, exp_w_gate, exp_w_up, exp_w_down, i)
        return _combine(plan, ls, wn, ys, base, modi, fnw, tm=tm, lrows=lrows, seq=n, final=final)

    w_out = ev_w_out[0].astype(BF16)
    tail0 = _mix0(o_a, o_b, x2, mod0, w_out[:CONV_CH], w_out[CONV_CH:], tail_weights(0), tm=tm, seq=n)
    h = moe(0, *tail0, mod0, False)

    bq, z = _inproj1(h, mod1, norm_mix_w[1].reshape(1, d), od_w_in[0].astype(BF16), tm=tm, seq=n)
    tail1 = _mix1(bq, z, h, mod1, od_conv_w[0].reshape(3, d), od_w_out[0].astype(BF16), tail_weights(1), tm=tm, seq=n)
    out = moe(1, *tail1, mod1, True)
    return out.reshape(bsz, n, d)
```
